```python
import jax, jax.numpy as jnp
from jax import lax
import numpy as np

D_MODEL = 2048
BATCH = 4
SEQ = 2048
DEPTH = 2
DEC_BATCH = 128
DEC_SEQ = 1
PAST_LEN = 8192
PAGE_SIZE = 128

N_META = 16
MIX_WIDTH = D_MODEL
H_RET = 8
RET_DK = 128
RET_DV = (MIX_WIDTH // 2) // H_RET
RET_WIDTH = H_RET * RET_DV
RET_CHUNK = 128
H_MLA = 8
MLA_V = (MIX_WIDTH - RET_WIDTH) // H_MLA
MLA_WIDTH = H_MLA * MLA_V
QK_NOPE = 128
QK_ROPE = 64
Q_RANK = 512
KV_RANK = 512
ATTN_BLOCK = 128
SM_SCALE = (QK_NOPE + QK_ROPE) ** -0.5
N_EXPERTS = 16
N_GROUPS = 4
EXPERTS_PER_GROUP = N_EXPERTS // N_GROUPS
TOP_K = 2
D_EXPERT = 512
ROPE_BASE = 10000.0
NORM_EPS = 1e-5
ALPHA = (2 * DEPTH) ** 0.25
BETA = (8 * DEPTH) ** -0.25
IN_SIZES = (H_RET * RET_DK, H_RET * RET_DK, RET_WIDTH, RET_WIDTH, Q_RANK, KV_RANK, QK_ROPE)
N_IN = H_RET * RET_DK * 2 + RET_WIDTH * 2 + Q_RANK + KV_RANK + QK_ROPE

kernel_name = "hymba_retention_mla_shared_router_moe_step"


def layer_norm(x, g, b):
    xf = x.astype(jnp.float32)
    mu = jnp.mean(xf, -1, keepdims=True)
    var = jnp.mean(jnp.square(xf - mu), -1, keepdims=True)
    y = (xf - mu) * lax.rsqrt(var + NORM_EPS)
    return (y * g.astype(jnp.float32) + b.astype(jnp.float32)).astype(x.dtype)


def rms_norm(x, g):
    xf = x.astype(jnp.float32)
    y = xf * lax.rsqrt(jnp.mean(jnp.square(xf), -1, keepdims=True) + NORM_EPS)
    return (y * g.astype(jnp.float32)).astype(x.dtype)


def rope(x, pos):
    d = x.shape[-1]
    half = d // 2
    inv = ROPE_BASE ** (-jnp.arange(half, dtype=jnp.float32) / half)
    ang = pos.astype(jnp.float32)[:, None] * inv[None, :]
    shape = (1, pos.shape[0]) + (1,) * (x.ndim - 3) + (half,)
    cos = jnp.cos(ang).reshape(shape)
    sin = jnp.sin(ang).reshape(shape)
    xf = x.astype(jnp.float32)
    x1, x2 = xf[..., :half], xf[..., half:]
    return jnp.concatenate([x1 * cos - x2 * sin, x2 * cos + x1 * sin], -1).astype(x.dtype)


def project(h, pos, w_in, q_norm_g, w_uq, kv_norm_g):
    B, T, _ = h.shape
    proj = h @ w_in
    parts, start = [], 0
    for size in IN_SIZES:
        parts.append(proj[..., start:start + size])
        start += size
    rq, rk, rv, rg, cq, ckv, kpe = parts
    rq = rope(rq.reshape(B, T, H_RET, RET_DK), pos)
    rk = rope(rk.reshape(B, T, H_RET, RET_DK), pos) * (RET_DK ** -0.5)
    rv = rv.reshape(B, T, H_RET, RET_DV)
    q = (rms_norm(cq, q_norm_g) @ w_uq).reshape(B, T, H_MLA, QK_NOPE + QK_ROPE)
    q_nope = q[..., :QK_NOPE]
    q_pe = rope(q[..., QK_NOPE:], pos)
    ckv = rms_norm(ckv, kv_norm_g)
    kpe = rope(kpe, pos)
    return rq, rk, rv, rg, q_nope, q_pe, ckv, kpe


def retention_chunk(S, q, k, v, log_gamma):
    C = q.shape[2]
    q, k, v, S = (a.astype(jnp.float32) for a in (q, k, v, S))
    i = jnp.arange(C, dtype=jnp.float32)
    rel = i[:, None] - i[None, :]
    lg = log_gamma[:, None, None]
    dmat = jnp.where(rel[None] >= 0, jnp.exp(jnp.maximum(rel, 0.0)[None] * lg), 0.0)
    inner = jnp.einsum('bhij,bhjv->bhiv', jnp.einsum('bhid,bhjd->bhij', q, k) * dmat, v)
    cross_decay = jnp.exp(log_gamma[:, None] * (i + 1.0)[None, :])[:, :, None]
    cross = jnp.einsum('bhid,bhdv->bhiv', q, S) * cross_decay
    k_dec = k * jnp.exp(log_gamma[:, None] * (C - 1.0 - i)[None, :])[:, :, None]
    S_new = jnp.exp(log_gamma * C)[:, None, None] * S + jnp.einsum('bhjd,bhjv->bhdv', k_dec, v)
    return inner + cross, S_new


def retention_prompt(q, k, v, log_gamma):
    q, k, v = (jnp.transpose(a, (0, 2, 1, 3)) for a in (q, k, v))
    B, H, T, _ = q.shape
    S0 = jnp.zeros((B, H, RET_DK, RET_DV), jnp.float32)
    o_meta, S1 = retention_chunk(S0, q[:, :, :N_META], k[:, :, :N_META], v[:, :, :N_META], log_gamma)
    n_chunks = (T - N_META) // RET_CHUNK

    def to_chunks(a):
        return jnp.moveaxis(a[:, :, N_META:].reshape(B, H, n_chunks, RET_CHUNK, a.shape[-1]), 2, 0)

    def step(S, blk):
        o, S = retention_chunk(S, blk[0], blk[1], blk[2], log_gamma)
        return S, o

    S_fin, o_c = lax.scan(step, S1, (to_chunks(q), to_chunks(k), to_chunks(v)))
    o_real = jnp.moveaxis(o_c, 0, 2).reshape(B, H, T - N_META, RET_DV)
    return jnp.concatenate([o_meta, o_real], 2), S_fin


def retention_output(o, g, gn_g, gn_b):
    mu = jnp.mean(o, -1, keepdims=True)
    var = jnp.mean(jnp.square(o - mu), -1, keepdims=True)
    on = (o - mu) * lax.rsqrt(var + NORM_EPS)
    B, H, T, DV = o.shape
    on = jnp.transpose(on, (0, 2, 1, 3)).reshape(B, T, H * DV)
    on = on * gn_g.astype(jnp.float32) + gn_b.astype(jnp.float32)
    return (jax.nn.silu(g.astype(jnp.float32)) * on).astype(g.dtype)


def attend(qn, qp, kn, kp, v, q_pos, k_pos):
    s = (jnp.einsum('bqhd,bkhd->bhqk', qn, kn) + jnp.einsum('bqhd,bkd->bhqk', qp, kp)).astype(jnp.float32) * SM_SCALE
    s = jnp.where((k_pos[None, :] <= q_pos[:, None])[None, None], s, -jnp.inf)
    p = jax.nn.softmax(s, axis=-1).astype(v.dtype)
    return jnp.einsum('bhqk,bkhd->bqhd', p, v)


def mla_prompt(q_nope, q_pe, ckv, kpe, w_ukv):
    B, T, _ = ckv.shape
    kv = (ckv @ w_ukv).reshape(B, T, H_MLA, QK_NOPE + MLA_V)
    k_nope, v = kv[..., :QK_NOPE], kv[..., QK_NOPE:]
    pos = jnp.arange(T)
    o_meta = attend(q_nope[:, :N_META], q_pe[:, :N_META], k_nope[:, :N_META], kpe[:, :N_META],
                    v[:, :N_META], pos[:N_META], pos[:N_META])
    n_blocks = (T - N_META) // ATTN_BLOCK

    def block(i):
        start = N_META + i * ATTN_BLOCK
        qn = lax.dynamic_slice_in_dim(q_nope, start, ATTN_BLOCK, axis=1)
        qp = lax.dynamic_slice_in_dim(q_pe, start, ATTN_BLOCK, axis=1)
        q_pos = start + jnp.arange(ATTN_BLOCK)
        return attend(qn, qp, k_nope, kpe, v, q_pos, pos)

    o_blocks = lax.map(block, jnp.arange(n_blocks))
    o_real = jnp.moveaxis(o_blocks, 0, 1).reshape(B, T - N_META, H_MLA, MLA_V)
    return jnp.concatenate([o_meta, o_real], 1).reshape(B, T, MLA_WIDTH)


def mla_sample(q_nope, q_pe, ckv_new, kpe_new, ckv_past, kpe_past, w_ukv):
    B, S = q_nope.shape[:2]
    P = ckv_past.shape[1]
    w = w_ukv.reshape(KV_RANK, H_MLA, QK_NOPE + MLA_V)
    w_uk, w_uv = w[..., :QK_NOPE], w[..., QK_NOPE:]
    q_lat = jnp.einsum('bshd,rhd->bshr', q_nope, w_uk)
    s_past = jnp.einsum('bshr,btr->bhst', q_lat, ckv_past) + jnp.einsum('bshd,btd->bhst', q_pe, kpe_past)
    s_new = jnp.einsum('bshr,btr->bhst', q_lat, ckv_new) + jnp.einsum('bshd,btd->bhst', q_pe, kpe_new)
    causal = jnp.arange(S)[None, :] <= jnp.arange(S)[:, None]
    s_new = jnp.where(causal[None, None], s_new.astype(jnp.float32) * SM_SCALE, -jnp.inf)
    s = jnp.concatenate([s_past.astype(jnp.float32) * SM_SCALE, s_new], -1)
    p = jax.nn.softmax(s, axis=-1).astype(ckv_past.dtype)
    o_lat = (jnp.einsum('bhst,btr->bshr', p[..., :P], ckv_past)
             + jnp.einsum('bhst,btr->bshr', p[..., P:], ckv_new))
    return jnp.einsum('bshr,rhd->bshd', o_lat, w_uv).reshape(B, S, MLA_WIDTH)


def mixer_output(ret_o, mla_o, w_o):
    return jnp.concatenate([ret_o.astype(mla_o.dtype), mla_o], -1) @ w_o


def moe(h, w_router, router_bias, w1, w3, w2):
    B, T, D = h.shape
    hf = h.reshape(B * T, D)
    scores = jax.nn.sigmoid((hf @ w_router).astype(jnp.float32))
    sel = scores + router_bias.astype(jnp.float32)
    grp = sel.reshape(-1, N_GROUPS, EXPERTS_PER_GROUP)
    grp_score = jnp.sum(lax.top_k(grp, TOP_K)[0], -1)
    best = jnp.argmax(grp_score, -1)
    in_grp = (jnp.arange(N_EXPERTS) // EXPERTS_PER_GROUP)[None, :] == best[:, None]
    _, idx = lax.top_k(jnp.where(in_grp, sel, -jnp.inf), TOP_K)
    wts = jnp.take_along_axis(scores, idx, -1)
    wts = wts / jnp.sum(wts, -1, keepdims=True)
    gates = jnp.sum(jax.nn.one_hot(idx, N_EXPERTS, dtype=jnp.float32) * wts[..., None], axis=1)
    hid = jax.nn.silu(jnp.einsum('nd,edf->nef', hf, w1)) * jnp.einsum('nd,edf->nef', hf, w3)
    out = jnp.einsum('nef,efd->nd', hid * gates[..., None].astype(hid.dtype), w2)
    return out.reshape(B, T, D).astype(h.dtype)


def setup_inputs(seed: int = 0) -> dict:
    key = jax.random.key(seed)
    ks = jax.random.split(key, 32)
    f32 = jnp.float32
    n_pages = PAST_LEN // PAGE_SIZE
    n_used = DEC_BATCH * n_pages
    n_pool = n_used + (n_used + 3) // 4

    def nrm(k, shape, scale=1.0):
        return jax.random.normal(k, shape, f32) * scale

    page_table = jax.random.permutation(ks[5], n_pool)[:n_used].reshape(DEC_BATCH, n_pages).astype(jnp.int32)
    return {
        "x_prompt": nrm(ks[0], (BATCH, SEQ, D_MODEL)),
        "x_sample": nrm(ks[1], (DEC_BATCH, DEC_SEQ, D_MODEL)),
        "cache_ckv": nrm(ks[2], (DEPTH, n_pool, PAGE_SIZE, KV_RANK)),
        "cache_kpe": nrm(ks[3], (DEPTH, n_pool, PAGE_SIZE, QK_ROPE)),
        "state_ret": nrm(ks[4], (DEPTH, DEC_BATCH, H_RET, RET_DK, RET_DV)),
        "page_table": page_table,
        "meta_tokens": nrm(ks[6], (N_META, D_MODEL)),
        "ln0_g": 1.0 + nrm(ks[7], (D_MODEL,), 0.05),
        "ln0_b": nrm(ks[8], (D_MODEL,), 0.02),
        "w_in": nrm(ks[9], (DEPTH, D_MODEL, N_IN), D_MODEL ** -0.5),
        "q_norm_g": 1.0 + nrm(ks[10], (DEPTH, Q_RANK), 0.05),
        "w_uq": nrm(ks[11], (DEPTH, Q_RANK, H_MLA * (QK_NOPE + QK_ROPE)), Q_RANK ** -0.5),
        "kv_norm_g": 1.0 + nrm(ks[12], (DEPTH, KV_RANK), 0.05),
        "w_ukv": nrm(ks[13], (DEPTH, KV_RANK, H_MLA * (QK_NOPE + MLA_V)), KV_RANK ** -0.5),
        "ret_gn_g": 1.0 + nrm(ks[14], (DEPTH, RET_WIDTH), 0.05),
        "ret_gn_b": nrm(ks[15], (DEPTH, RET_WIDTH), 0.02),
        "w_o": nrm(ks[16], (DEPTH, MIX_WIDTH, D_MODEL), BETA * MIX_WIDTH ** -0.5),
        "ln1_g": 1.0 + nrm(ks[17], (DEPTH, D_MODEL), 0.05),
        "ln1_b": nrm(ks[18], (DEPTH, D_MODEL), 0.02),
        "w_router": nrm(ks[19], (D_MODEL, N_EXPERTS), D_MODEL ** -0.5),
        "router_bias": nrm(ks[20], (N_EXPERTS,), 0.01),
        "w1": nrm(ks[21], (DEPTH, N_EXPERTS, D_MODEL, D_EXPERT), D_MODEL ** -0.5),
        "w3": nrm(ks[22], (DEPTH, N_EXPERTS, D_MODEL, D_EXPERT), D_MODEL ** -0.5),
        "w2": nrm(ks[23], (DEPTH, N_EXPERTS, D_EXPERT, D_MODEL), BETA * D_EXPERT ** -0.5),
        "ln2_g": 1.0 + nrm(ks[24], (DEPTH, D_MODEL), 0.05),
        "ln2_b": nrm(ks[25], (DEPTH, D_MODEL), 0.02),
    }


def reference(x_prompt, x_sample, cache_ckv, cache_kpe, state_ret, page_table, meta_tokens,
              ln0_g, ln0_b, w_in, q_norm_g, w_uq, kv_norm_g, w_ukv, ret_gn_g, ret_gn_b, w_o,
              ln1_g, ln1_b, w_router, router_bias, w1, w3, w2, ln2_g, ln2_b):
    B = x_prompt.shape[0]
    Bd, S = x_sample.shape[0], x_sample.shape[1]
    log_gamma = jnp.log1p(-jnp.exp2(-5.0 - jnp.arange(H_RET, dtype=jnp.float32)))

    meta = jnp.broadcast_to(meta_tokens[None].astype(x_prompt.dtype), (B, N_META, D_MODEL))
    xp = jnp.concatenate([meta, x_prompt], 1)
    T = xp.shape[1]
    pos_p = jnp.arange(T)
    pos_s = PAST_LEN + jnp.arange(S)
    hp = layer_norm(xp, ln0_g, ln0_b)
    hs = layer_norm(x_sample, ln0_g, ln0_b)

    ckv_p_l, kpe_p_l, ret_p_l, ckv_s_l, kpe_s_l, ret_s_l = [], [], [], [], [], []
    for l in range(DEPTH):
        rq, rk, rv, rg, qn, qp, ckv, kpe = project(hp, pos_p, w_in[l], q_norm_g[l], w_uq[l], kv_norm_g[l])
        ret_o, ret_S = retention_prompt(rq, rk, rv, log_gamma)
        ret_o = retention_output(ret_o, rg, ret_gn_g[l], ret_gn_b[l])
        mla_o = mla_prompt(qn, qp, ckv, kpe, w_ukv[l])
        hp = layer_norm(ALPHA * hp + mixer_output(ret_o, mla_o, w_o[l]), ln1_g[l], ln1_b[l])
        hp = layer_norm(ALPHA * hp + moe(hp, w_router, router_bias, w1[l], w3[l], w2[l]), ln2_g[l], ln2_b[l])
        ckv_p_l.append(ckv)
        kpe_p_l.append(kpe)
        ret_p_l.append(ret_S)

        rq, rk, rv, rg, qn, qp, ckv, kpe = project(hs, pos_s, w_in[l], q_norm_g[l], w_uq[l], kv_norm_g[l])
        ret_o, ret_S = retention_chunk(state_ret[l], jnp.transpose(rq, (0, 2, 1, 3)),
                                       jnp.transpose(rk, (0, 2, 1, 3)), jnp.transpose(rv, (0, 2, 1, 3)), log_gamma)
        ret_o = retention_output(ret_o, rg, ret_gn_g[l], ret_gn_b[l])
        ckv_past = cache_ckv[l][page_table].reshape(Bd, PAST_LEN, KV_RANK)
        kpe_past = cache_kpe[l][page_table].reshape(Bd, PAST_LEN, QK_ROPE)
        mla_o = mla_sample(qn, qp, ckv, kpe, ckv_past, kpe_past, w_ukv[l])
        hs = layer_norm(ALPHA * hs + mixer_output(ret_o, mla_o, w_o[l]), ln1_g[l], ln1_b[l])
        hs = layer_norm(ALPHA * hs + moe(hs, w_router, router_bias, w1[l], w3[l], w2[l]), ln2_g[l], ln2_b[l])
        ckv_s_l.append(ckv)
        kpe_s_l.append(kpe)
        ret_s_l.append(ret_S)

    y_prompt = hp[:, N_META:]
    y_sample = hs
    new_ckv_prompt = jnp.stack(ckv_p_l, 0)
    new_kpe_prompt = jnp.stack(kpe_p_l, 0)
    new_ret_prompt = jnp.stack(ret_p_l, 0)
    new_ckv_sample = jnp.stack(ckv_s_l, 0)
    new_kpe_sample = jnp.stack(kpe_s_l, 0)
    new_ret_sample = jnp.stack(ret_s_l, 0)
    return (y_prompt, y_sample, new_ckv_prompt, new_kpe_prompt, new_ret_prompt,
            new_ckv_sample, new_kpe_sample, new_ret_sample)
```

```python
import functools
import math

import jax
import jax.numpy as jnp
from jax import lax
from jax.experimental import pallas as pl
from jax.experimental.pallas import tpu as pltpu

D_MODEL = 2048
N_META = 16
PAGE_SIZE = 128
H_RET = 8
RET_DK = 128
RET_DV = 128
RET_WIDTH = H_RET * RET_DV
RET_CHUNK = 128
H_MLA = 8
MLA_V = 128
MLA_WIDTH = H_MLA * MLA_V
QK_NOPE = 128
QK_ROPE = 64
Q_RANK = 512
KV_RANK = 512
SM_SCALE = (QK_NOPE + QK_ROPE) ** -0.5
N_EXPERTS = 16
N_GROUPS = 4
EXPERTS_PER_GROUP = N_EXPERTS // N_GROUPS
D_EXPERT = 512
ROPE_BASE = 10000.0
NORM_EPS = 1e-5
DEPTH = 2
ALPHA = (2 * DEPTH) ** 0.25
R_COLS = 2 * H_RET * RET_DK + 2 * RET_WIDTH
C_COLS = Q_RANK + KV_RANK
LANES = 128
HEAD_PAD = 2 * LANES
VMEM_LIMIT = 56 * 1024 * 1024

BF16 = jnp.bfloat16
F32 = jnp.float32


def _tile(n, cap, align=16):
    best = None
    for t in range(align, min(n, cap) + 1, align):
        if n % t == 0:
            best = t
    return best if best is not None else n


def _params(*sem):
    return pltpu.CompilerParams(dimension_semantics=sem, vmem_limit_bytes=VMEM_LIMIT)


def _dot(a, b):
    return jnp.dot(a, b, preferred_element_type=F32)


def _dot_nt(a, b):
    return lax.dot_general(a, b, (((1,), (1,)), ((), ())), preferred_element_type=F32)


def _dot_tn(a, b):
    return lax.dot_general(a, b, (((0,), (0,)), ((), ())), preferred_element_type=F32)


def _layer_norm(x, g, b):
    mu = jnp.mean(x, -1, keepdims=True)
    xc = x - mu
    var = jnp.mean(xc * xc, -1, keepdims=True)
    return xc * lax.rsqrt(var + NORM_EPS) * g + b


def _rms_norm(x, g):
    return x * lax.rsqrt(jnp.mean(x * x, -1, keepdims=True) + NORM_EPS) * g


def _silu(x):
    return x * (1.0 / (1.0 + jnp.exp(-x)))


def _rope_half(x, c, s):
    return x * c + pltpu.roll(x, LANES // 2, 1) * s


def _rope_pad(u, c, a, b):
    return u * c + pltpu.roll(u, LANES - QK_ROPE // 2, 1) * a + pltpu.roll(u, QK_ROPE // 2, 1) * b


def _ln_kernel(x_ref, g_ref, b_ref, o_ref, ob_ref):
    y = _layer_norm(x_ref[...], g_ref[...], b_ref[...])
    o_ref[...] = y
    ob_ref[...] = y.astype(BF16)


def _ln(x, g, b):
    n, d = x.shape
    tm = _tile(n, 1024)
    row = pl.BlockSpec((tm, d), lambda i: (i, 0))
    vec = pl.BlockSpec((1, d), lambda i: (0, 0))
    return pl.pallas_call(
        _ln_kernel, grid=(n // tm,), in_specs=[row, vec, vec], out_specs=[row, row],
        out_shape=[jax.ShapeDtypeStruct((n, d), F32), jax.ShapeDtypeStruct((n, d), BF16)],
        compiler_params=_params("parallel"), name="ln0")(x, g.reshape(1, d), b.reshape(1, d))


def _mm_kernel(x_ref, w_ref, o_ref):
    o_ref[...] = _dot(x_ref[...], w_ref[...])


def _mm(x, w, tn, name):
    n, k = x.shape
    nw = w.shape[1]
    tm = _tile(n, 1024)
    return pl.pallas_call(
        _mm_kernel, grid=(nw // tn, n // tm),
        in_specs=[pl.BlockSpec((tm, k), lambda j, i: (i, 0)), pl.BlockSpec((k, tn), lambda j, i: (0, j))],
        out_specs=pl.BlockSpec((tm, tn), lambda j, i: (i, j)),
        out_shape=jax.ShapeDtypeStruct((n, nw), F32),
        compiler_params=_params("parallel", "parallel"), name=name)(x, w)


def _mla_proj_kernel(x_ref, rc_ref, ra_ref, rb_ref, qg_ref, kg_ref, wuq_ref, wukv_ref,
                     q_ref, k_ref, v_ref, ckv_ref, kpe_ref):
    x = x_ref[...]
    rc, ra, rb = rc_ref[...], ra_ref[...], rb_ref[...]
    cq = _rms_norm(x[:, :Q_RANK], qg_ref[...]).astype(BF16)
    ckv = _rms_norm(x[:, Q_RANK:C_COLS], kg_ref[...])
    kpe = _rope_pad(x[:, C_COLS:], rc, ra, rb)
    ckv_ref[...] = ckv
    kpe_ref[...] = kpe
    kpe_b = kpe.astype(BF16)
    q = _dot(cq, wuq_ref[...])
    kv = _dot(ckv.astype(BF16), wukv_ref[...])
    for h in range(H_MLA):
        lo = h * HEAD_PAD
        q_ref[:, lo:lo + LANES] = q[:, lo:lo + LANES].astype(BF16)
        q_ref[:, lo + LANES:lo + HEAD_PAD] = _rope_pad(q[:, lo + LANES:lo + HEAD_PAD], rc, ra, rb).astype(BF16)
        k_ref[:, lo:lo + LANES] = kv[:, lo:lo + LANES].astype(BF16)
        k_ref[:, lo + LANES:lo + HEAD_PAD] = kpe_b
        v_ref[:, h * MLA_V:(h + 1) * MLA_V] = kv[:, lo + LANES:lo + HEAD_PAD].astype(BF16)


def _mla_proj(proj_c, rope_c, rope_a, rope_b, qg, kg, wuq, wukv):
    n, t = proj_c.shape[0], rope_c.shape[0]
    tm = _tile(t, 1024)
    row = lambda w: pl.BlockSpec((tm, w), lambda i: (i, 0))
    tab = pl.BlockSpec((tm, LANES), lambda i: (i % (t // tm), 0))
    full = lambda a: pl.BlockSpec(a.shape, lambda i: (0,) * a.ndim)
    return pl.pallas_call(
        _mla_proj_kernel, grid=(n // tm,),
        in_specs=[row(C_COLS + LANES), tab, tab, tab, full(qg), full(kg), full(wuq), full(wukv)],
        out_specs=[row(H_MLA * HEAD_PAD), row(H_MLA * HEAD_PAD), row(MLA_WIDTH), row(KV_RANK), row(LANES)],
        out_shape=[jax.ShapeDtypeStruct((n, H_MLA * HEAD_PAD), BF16), jax.ShapeDtypeStruct((n, H_MLA * HEAD_PAD), BF16),
                   jax.ShapeDtypeStruct((n, MLA_WIDTH), BF16), jax.ShapeDtypeStruct((n, KV_RANK), F32),
                   jax.ShapeDtypeStruct((n, LANES), F32)],
        compiler_params=_params("parallel"), name="mla_proj")(proj_c, rope_c, rope_a, rope_b, qg, kg, wuq, wukv)


def _ret_chunk(s, q, k, v, dmat, cdec, kdec, gc):
    qb, vb = q.astype(BF16), v.astype(BF16)
    inner = _dot((_dot_nt(qb, k.astype(BF16)) * dmat).astype(BF16), vb)
    cross = _dot(qb, s.astype(BF16)) * cdec
    s_new = gc * s + _dot_tn((k * kdec).astype(BF16), vb)
    return inner + cross, s_new


def _ret_out(o, g, gn_g, gn_b):
    mu = jnp.mean(o, -1, keepdims=True)
    oc = o - mu
    var = jnp.mean(oc * oc, -1, keepdims=True)
    on = oc * lax.rsqrt(var + NORM_EPS) * gn_g + gn_b
    return (_silu(g) * on).astype(BF16)


def _ret_kernel(q_ref, k_ref, v_ref, g_ref, c_ref, s_ref, dmat_ref, cdec_ref, kdec_ref, gc_ref, gng_ref, gnb_ref,
                o_ref, st_ref, pad_ref, *, n_chunks):
    dmat, cdec, kdec, gc = dmat_ref[...], cdec_ref[...], kdec_ref[...], gc_ref[...]
    gn_g, gn_b = gng_ref[...], gnb_ref[...]
    kscale = RET_DK ** -0.5

    def load(r0, rows):
        sl = pl.ds(r0, rows)
        c, s = c_ref[sl, :], s_ref[sl, :]
        return (_rope_half(q_ref[sl, :], c, s), _rope_half(k_ref[sl, :], c, s) * kscale, v_ref[sl, :], g_ref[sl, :])

    q0, k0, v0, g0 = load(0, N_META)
    pad_ref[...] = jnp.zeros(pad_ref.shape, F32)
    lead = RET_CHUNK - N_META
    pad_ref[0, lead:, :] = q0
    pad_ref[1, lead:, :] = k0
    pad_ref[2, lead:, :] = v0
    o0, s1 = _ret_chunk(jnp.zeros((RET_DK, RET_DV), F32), pad_ref[0], pad_ref[1], pad_ref[2], dmat, cdec, kdec, gc)
    o_ref[0:N_META, :] = _ret_out(o0[lead:, :], g0, gn_g, gn_b)

    def body(c, s):
        r0 = pl.multiple_of(N_META + c * RET_CHUNK, 16)
        q, k, v, g = load(r0, RET_CHUNK)
        o, s = _ret_chunk(s, q, k, v, dmat, cdec, kdec, gc)
        o_ref[pl.ds(r0, RET_CHUNK), :] = _ret_out(o, g, gn_g, gn_b)
        return s

    st_ref[...] = lax.fori_loop(0, n_chunks, body, s1)


def _retention_prompt(proj_r, cos, sin, consts, gn_g, gn_b, b, t):
    dmat, cdec, kdec, gc = consts
    x = proj_r.reshape(b, t, R_COLS)
    head = lambda off: pl.BlockSpec((None, t, LANES), lambda i, h: (i, 0, off + h))
    tab = pl.BlockSpec((t, LANES), lambda i, h: (0, 0))
    per_h = lambda r: pl.BlockSpec((None, r, LANES), lambda i, h: (h, 0, 0))
    return pl.pallas_call(
        functools.partial(_ret_kernel, n_chunks=(t - N_META) // RET_CHUNK), grid=(b, H_RET),
        in_specs=[head(0), head(H_RET), head(2 * H_RET), head(3 * H_RET), tab, tab,
                  per_h(RET_CHUNK), per_h(RET_CHUNK), per_h(RET_CHUNK), per_h(1), per_h(1), per_h(1)],
        out_specs=[pl.BlockSpec((None, t, LANES), lambda i, h: (i, 0, h)),
                   pl.BlockSpec((None, None, RET_DK, RET_DV), lambda i, h: (i, h, 0, 0))],
        out_shape=[jax.ShapeDtypeStruct((b, t, RET_WIDTH), BF16), jax.ShapeDtypeStruct((b, H_RET, RET_DK, RET_DV), F32)],
        scratch_shapes=[pltpu.VMEM((3, RET_CHUNK, LANES), F32)],
        compiler_params=_params("parallel", "parallel"), name="retention_prompt",
    )(x, x, x, x, cos, sin, dmat, cdec, kdec, gc, gn_g.reshape(H_RET, 1, RET_DV), gn_b.reshape(H_RET, 1, RET_DV))


def _attn_kernel(q_ref, k_ref, v_ref, o_ref, *, t, blk):
    n_full = t // blk
    starts = [(i * blk, blk) for i in range(n_full)]
    if t % blk:
        starts.append((n_full * blk, t % blk))
    for qi, (q0, qn) in enumerate(starts):
        q = q_ref[q0:q0 + qn, :]
        m = jnp.full((qn, 1), -jnp.inf, F32)
        l = jnp.zeros((qn, 1), F32)
        acc = jnp.zeros((qn, MLA_V), F32)
        for ki, (k0, kn) in enumerate(starts[:qi + 1]):
            s = _dot_nt(q, k_ref[k0:k0 + kn, :]) * SM_SCALE
            if ki == qi:
                rows = lax.broadcasted_iota(jnp.int32, (qn, kn), 0)
                cols = lax.broadcasted_iota(jnp.int32, (qn, kn), 1)
                s = jnp.where(cols <= rows, s, -jnp.inf)
            m_new = jnp.maximum(m, jnp.max(s, -1, keepdims=True))
            p = jnp.exp(s - m_new)
            a = jnp.exp(m - m_new)
            l = a * l + jnp.sum(p, -1, keepdims=True)
            acc = a * acc + _dot(p.astype(BF16), v_ref[k0:k0 + kn, :])
            m = m_new
        o_ref[q0:q0 + qn, :] = (acc / l).astype(BF16)


def _attention_prompt(q, k, v, b, t):
    blk = 512 if t >= 512 else 128
    spec = lambda w: pl.BlockSpec((None, t, w), lambda i, h: (i, 0, h))
    return pl.pallas_call(
        functools.partial(_attn_kernel, t=t, blk=blk), grid=(b, H_MLA),
        in_specs=[spec(HEAD_PAD), spec(HEAD_PAD), spec(MLA_V)], out_specs=spec(MLA_V),
        out_shape=jax.ShapeDtypeStruct((b, t, MLA_WIDTH), BF16),
        compiler_params=_params("parallel", "parallel"), name="attention_prompt",
    )(q.reshape(b, t, -1), k.reshape(b, t, -1), v.reshape(b, t, -1))


def _route(h, wr_hi, wr_lo, bias):
    hi = h.astype(BF16)
    lo = (h - hi.astype(F32)).astype(BF16)
    logits = _dot(hi, wr_hi) + (_dot(lo, wr_hi) + _dot(hi, wr_lo))
    scores = 1.0 / (1.0 + jnp.exp(-logits))
    sel = scores + bias
    lane = lax.broadcasted_iota(jnp.int32, sel.shape, 1).astype(F32)
    neg = -jnp.inf

    def top2(vals):
        t1 = jnp.max(vals, -1, keepdims=True)
        i1 = jnp.min(jnp.where(vals == t1, lane, float(N_EXPERTS)), -1, keepdims=True)
        rest = jnp.where(lane == i1, neg, vals)
        t2 = jnp.max(rest, -1, keepdims=True)
        i2 = jnp.min(jnp.where(rest == t2, lane, float(N_EXPERTS)), -1, keepdims=True)
        return t1, i1, t2, i2

    def in_group(g):
        lo = g * float(EXPERTS_PER_GROUP)
        return jnp.where(lane >= lo, lane, float(N_EXPERTS)) < lo + EXPERTS_PER_GROUP

    best_score = best = None
    for g in range(N_GROUPS):
        t1, _, t2, _ = top2(jnp.where(in_group(float(g)), sel, neg))
        gs = t1 + t2
        if g == 0:
            best_score, best = gs, jnp.zeros_like(gs)
        else:
            upd = gs > best_score
            best_score = jnp.where(upd, gs, best_score)
            best = jnp.where(upd, float(g), best)
    in_best = in_group(best)
    _, i1, _, i2 = top2(jnp.where(in_best, sel, neg))
    s1 = jnp.sum(jnp.where(lane == i1, scores, 0.0), -1, keepdims=True)
    s2 = jnp.sum(jnp.where(lane == i2, scores, 0.0), -1, keepdims=True)
    tot = s1 + s2
    return jnp.where(lane == i1, s1 / tot, 0.0) + jnp.where(lane == i2, s2 / tot, 0.0)


def _oproj_kernel(r_ref, a_ref, h_ref, wa_ref, wb_ref, g_ref, b_ref, whi_ref, wlo_ref, rb_ref,
                  o_ref, ob_ref, gate_ref):
    y = _dot(r_ref[...], wa_ref[...]) + _dot(a_ref[...], wb_ref[...])
    h1 = _layer_norm(ALPHA * h_ref[...] + y, g_ref[...], b_ref[...])
    o_ref[...] = h1
    ob_ref[...] = h1.astype(BF16)
    gate_ref[...] = _route(h1, whi_ref[...], wlo_ref[...], rb_ref[...])


def _oproj(ret_o, mla_o, h, w_o, g, b, wr_hi, wr_lo, rbias):
    n, d = h.shape
    tm = _tile(n, 512)
    half = w_o.shape[0] // 2
    row = lambda w: pl.BlockSpec((tm, w), lambda i: (i, 0))
    vec = lambda w: pl.BlockSpec((1, w), lambda i: (0, 0))
    wspec = lambda j: pl.BlockSpec((half, d), lambda i: (j, 0))
    rspec = pl.BlockSpec((d, N_EXPERTS), lambda i: (0, 0))
    return pl.pallas_call(
        _oproj_kernel, grid=(n // tm,),
        in_specs=[row(half), row(half), row(d), wspec(0), wspec(1), vec(d), vec(d), rspec, rspec, vec(N_EXPERTS)],
        out_specs=[row(d), row(d), row(N_EXPERTS)],
        out_shape=[jax.ShapeDtypeStruct((n, d), F32), jax.ShapeDtypeStruct((n, d), BF16),
                   jax.ShapeDtypeStruct((n, N_EXPERTS), F32)],
        compiler_params=_params("parallel"), name="oproj_ln1_router",
    )(ret_o, mla_o, h, w_o, w_o, g.reshape(1, d), b.reshape(1, d), wr_hi, wr_lo, rbias.reshape(1, N_EXPERTS))


def _moe_kernel(x_ref, gate_ref, w1_ref, w3_ref, w2_ref, o_ref, acc_ref):
    e = pl.program_id(1)

    @pl.when(e == 0)
    def _():
        acc_ref[...] = jnp.zeros(acc_ref.shape, F32)

    x = x_ref[...]
    gates = gate_ref[...]
    lane = lax.broadcasted_iota(jnp.int32, gates.shape, 1)
    ge = jnp.sum(jnp.where(lane == e, gates, 0.0), -1, keepdims=True)
    hid = _silu(_dot(x, w1_ref[...])) * _dot(x, w3_ref[...])
    acc_ref[...] += _dot((hid * ge).astype(BF16), w2_ref[...])

    @pl.when(e == N_EXPERTS - 1)
    def _():
        o_ref[...] = acc_ref[...]


def _moe(xb, gates, w1, w3, w2):
    n, d = xb.shape
    tm = _tile(n, 1024)
    row = lambda w: pl.BlockSpec((tm, w), lambda i, e: (i, 0))
    return pl.pallas_call(
        _moe_kernel, grid=(n // tm, N_EXPERTS),
        in_specs=[row(d), row(N_EXPERTS),
                  pl.BlockSpec((None, d, D_EXPERT), lambda i, e: (e, 0, 0)),
                  pl.BlockSpec((None, d, D_EXPERT), lambda i, e: (e, 0, 0)),
                  pl.BlockSpec((None, D_EXPERT, d), lambda i, e: (e, 0, 0))],
        out_specs=row(d), out_shape=jax.ShapeDtypeStruct((n, d), F32),
        scratch_shapes=[pltpu.VMEM((tm, d), F32)],
        compiler_params=_params("parallel", "arbitrary"), name="moe")(xb, gates, w1, w3, w2)


def _res_ln_kernel(h_ref, y_ref, g_ref, b_ref, o_ref, ob_ref):
    z = _layer_norm(ALPHA * h_ref[...] + y_ref[...], g_ref[...], b_ref[...])
    o_ref[...] = z
    ob_ref[...] = z.astype(BF16)


def _res_ln(h, y, g, b):
    n, d = h.shape
    tm = _tile(n, 1024)
    row = pl.BlockSpec((tm, d), lambda i: (i, 0))
    vec = pl.BlockSpec((1, d), lambda i: (0, 0))
    return pl.pallas_call(
        _res_ln_kernel, grid=(n // tm,), in_specs=[row, row, vec, vec], out_specs=[row, row],
        out_shape=[jax.ShapeDtypeStruct((n, d), F32), jax.ShapeDtypeStruct((n, d), BF16)],
        compiler_params=_params("parallel"), name="residual_ln2")(h, y, g.reshape(1, d), b.reshape(1, d))


def _sample_proj_kernel(x_ref, rc_ref, ra_ref, rb_ref, qg_ref, kg_ref, wuq_ref, wuk_ref,
                        qlat_ref, qpe_ref, ckv_ref, kpe_ref):
    x = x_ref[...]
    rc, ra, rb = rc_ref[...], ra_ref[...], rb_ref[...]
    cq = _rms_norm(x[:, :Q_RANK], qg_ref[...]).astype(BF16)
    ckv_ref[...] = _rms_norm(x[:, Q_RANK:C_COLS], kg_ref[...])
    kpe_ref[...] = _rope_pad(x[:, C_COLS:], rc, ra, rb)
    q = _dot(cq, wuq_ref[...])
    for h in range(H_MLA):
        lo = h * HEAD_PAD
        qlat_ref[:, h * KV_RANK:(h + 1) * KV_RANK] = _dot(q[:, lo:lo + LANES].astype(BF16), wuk_ref[h]).astype(BF16)
        qpe_ref[:, h * LANES:(h + 1) * LANES] = _rope_pad(q[:, lo + LANES:lo + HEAD_PAD], rc, ra, rb).astype(BF16)


def _sample_proj(proj_c, rope_c, rope_a, rope_b, qg, kg, wuq, wuk_t):
    n = proj_c.shape[0]
    return pl.pallas_call(
        _sample_proj_kernel,
        out_shape=[jax.ShapeDtypeStruct((n, H_MLA * KV_RANK), BF16), jax.ShapeDtypeStruct((n, H_MLA * LANES), BF16),
                   jax.ShapeDtypeStruct((n, KV_RANK), F32), jax.ShapeDtypeStruct((n, LANES), F32)],
        compiler_params=pltpu.CompilerParams(vmem_limit_bytes=VMEM_LIMIT), name="sample_proj",
    )(proj_c, rope_c, rope_a, rope_b, qg, kg, wuq, wuk_t)


def _ret_step_kernel(x_ref, s_ref, c_ref, sn_ref, gc_ref, gng_ref, gnb_ref, o_ref, st_ref):
    x = x_ref[...]
    c, sn = c_ref[...], sn_ref[...]
    kscale = RET_DK ** -0.5
    w = H_RET * RET_DK
    for h in range(H_RET):
        q = _rope_half(x[:, h * RET_DK:(h + 1) * RET_DK], c, sn)
        k = _rope_half(x[:, w + h * RET_DK:w + (h + 1) * RET_DK], c, sn) * kscale
        v = x[:, 2 * w + h * RET_DV:2 * w + (h + 1) * RET_DV]
        g = x[:, 3 * w + h * RET_DV:3 * w + (h + 1) * RET_DV]
        q_col = jnp.broadcast_to(q, (RET_DK, RET_DK)).T
        k_col = jnp.broadcast_to(k, (RET_DK, RET_DK)).T
        s_new = gc_ref[h] * s_ref[h] + k_col * v
        st_ref[h] = s_new
        o = jnp.sum(q_col * s_new, 0, keepdims=True)
        o_ref[:, h * RET_DV:(h + 1) * RET_DV] = _ret_out(o, g, gng_ref[h], gnb_ref[h])


def _retention_step(layer, proj_r, state, cos, sin, gc, gn_g, gn_b):
    bd = proj_r.shape[0]
    full = lambda a: pl.BlockSpec(a.shape, lambda i: (0,) * a.ndim)
    gng, gnb = gn_g.reshape(H_RET, 1, RET_DV), gn_b.reshape(H_RET, 1, RET_DV)
    o, st = pl.pallas_call(
        _ret_step_kernel, grid=(bd,),
        in_specs=[pl.BlockSpec((None, 1, R_COLS), lambda i: (i, 0, 0)),
                  pl.BlockSpec((None, None, H_RET, RET_DK, RET_DV), lambda i: (layer, i, 0, 0, 0)),
                  full(cos), full(sin), full(gc), full(gng), full(gnb)],
        out_specs=[pl.BlockSpec((None, 1, RET_WIDTH), lambda i: (i, 0, 0)),
                   pl.BlockSpec((None, H_RET, RET_DK, RET_DV), lambda i: (i, 0, 0, 0))],
        out_shape=[jax.ShapeDtypeStruct((bd, 1, RET_WIDTH), BF16), jax.ShapeDtypeStruct(state.shape[1:], F32)],
        compiler_params=_params("parallel"), name="retention_step",
    )(proj_r.reshape(bd, 1, R_COLS), state, cos, sin, gc, gng, gnb)
    return o.reshape(bd, RET_WIDTH), st


def _decode_kernel(pt_ref, *refs, pages):
    ckv_refs, kpe_refs = refs[:pages], refs[pages:2 * pages]
    qlat_ref, qpe_ref, cnew_ref, knew_ref, o_ref, m_ref, l_ref, acc_ref = refs[2 * pages:]
    j = pl.program_id(1)

    @pl.when(j == 0)
    def _():
        m_ref[...] = jnp.full(m_ref.shape, -jnp.inf, F32)
        l_ref[...] = jnp.zeros(l_ref.shape, F32)
        acc_ref[...] = jnp.zeros(acc_ref.shape, F32)

    qlat = qlat_ref[...]
    qpe = qpe_ref[...][:, :QK_ROPE]

    def update(ckv, kpe):
        s = (_dot_nt(qlat, ckv) + _dot_nt(qpe, kpe)) * SM_SCALE
        m_old = m_ref[...]
        m_new = jnp.maximum(m_old, jnp.max(s, -1, keepdims=True))
        p = jnp.exp(s - m_new)
        a = jnp.exp(m_old - m_new)
        l_ref[...] = a * l_ref[...] + jnp.sum(p, -1, keepdims=True)
        acc_ref[...] = a * acc_ref[...] + _dot(p.astype(BF16), ckv)
        m_ref[...] = m_new

    for c_ref, k_ref in zip(ckv_refs, kpe_refs):
        update(c_ref[...].astype(BF16), k_ref[...].astype(BF16))

    @pl.when(j == pl.num_programs(1) - 1)
    def _():
        cn = jnp.broadcast_to(cnew_ref[...], (8, KV_RANK)).astype(BF16)
        kn = jnp.broadcast_to(knew_ref[...][:, :QK_ROPE], (8, QK_ROPE)).astype(BF16)
        s = (_dot_nt(qlat, cn) + _dot_nt(qpe, kn))[:, :1] * SM_SCALE
        m_old = m_ref[...]
        m_new = jnp.maximum(m_old, s)
        p = jnp.exp(s - m_new)
        a = jnp.exp(m_old - m_new)
        l = a * l_ref[...] + p
        acc = a * acc_ref[...] + p.astype(BF16).astype(F32) * cn[:1, :].astype(F32)
        o = (acc / l).astype(BF16)
        for h in range(H_MLA):
            o_ref[:, h * KV_RANK:(h + 1) * KV_RANK] = o[h:h + 1, :]


def _decode_attention(layer, page_table, cache_ckv, cache_kpe, qlat, qpe, ckv_new, kpe_new):
    bd, n_pages = page_table.shape
    pages = 8 if n_pages % 8 == 0 else 1
    ckv_spec = lambda p: pl.BlockSpec((None, None, PAGE_SIZE, KV_RANK),
                                      lambda i, j, pt: (layer, pt[i, j * pages + p], 0, 0))
    kpe_spec = lambda p: pl.BlockSpec((None, None, PAGE_SIZE, QK_ROPE),
                                      lambda i, j, pt: (layer, pt[i, j * pages + p], 0, 0))
    per_b = lambda r, w: pl.BlockSpec((None, r, w), lambda i, j, pt: (i, 0, 0))
    grid_spec = pltpu.PrefetchScalarGridSpec(
        num_scalar_prefetch=1, grid=(bd, n_pages // pages),
        in_specs=[ckv_spec(p) for p in range(pages)] + [kpe_spec(p) for p in range(pages)]
        + [per_b(H_MLA, KV_RANK), per_b(H_MLA, LANES), per_b(1, KV_RANK), per_b(1, LANES)],
        out_specs=per_b(1, H_MLA * KV_RANK),
        scratch_shapes=[pltpu.VMEM((H_MLA, 1), F32), pltpu.VMEM((H_MLA, 1), F32), pltpu.VMEM((H_MLA, KV_RANK), F32)])
    o = pl.pallas_call(
        functools.partial(_decode_kernel, pages=pages), grid_spec=grid_spec,
        out_shape=jax.ShapeDtypeStruct((bd, 1, H_MLA * KV_RANK), BF16),
        compiler_params=_params("parallel", "arbitrary"), name="decode_attention",
    )(page_table, *([cache_ckv] * pages), *([cache_kpe] * pages),
      qlat.reshape(bd, H_MLA, KV_RANK), qpe.reshape(bd, H_MLA, LANES),
      ckv_new.reshape(bd, 1, KV_RANK), kpe_new.reshape(bd, 1, LANES))
    return o.reshape(bd, H_MLA * KV_RANK)


def _uv_kernel(o_ref, w_ref, out_ref):
    for h in range(H_MLA):
        out_ref[:, h * MLA_V:(h + 1) * MLA_V] = _dot(o_ref[:, h * KV_RANK:(h + 1) * KV_RANK], w_ref[h]).astype(BF16)


def _uv_proj(o_lat, w_uv):
    bd = o_lat.shape[0]
    return pl.pallas_call(
        _uv_kernel, out_shape=jax.ShapeDtypeStruct((bd, MLA_WIDTH), BF16),
        compiler_params=pltpu.CompilerParams(vmem_limit_bytes=VMEM_LIMIT), name="uv_proj")(o_lat, w_uv)


def _rope_tables(pos, half):
    inv = ROPE_BASE ** (-jnp.arange(half, dtype=F32) / half)
    ang = pos.astype(F32)[:, None] * inv[None, :]
    return jnp.cos(ang), jnp.sin(ang)


def _rope_half_tables(pos):
    cos, sin = _rope_tables(pos, RET_DK // 2)
    return jnp.concatenate([cos, cos], -1), jnp.concatenate([-sin, sin], -1)


def _rope_pad_tables(pos):
    cos, sin = _rope_tables(pos, QK_ROPE // 2)
    z = jnp.zeros_like(cos)
    zz = jnp.zeros((pos.shape[0], LANES - QK_ROPE), F32)
    return (jnp.concatenate([cos, cos, zz], -1), jnp.concatenate([-sin, z, zz], -1),
            jnp.concatenate([z, sin, zz], -1))


def _retention_consts():
    lg = jnp.log1p(-jnp.exp2(-5.0 - jnp.arange(H_RET, dtype=F32)))
    i = jnp.arange(RET_CHUNK, dtype=F32)
    rel = i[:, None] - i[None, :]
    dmat = jnp.where(rel[None] >= 0, jnp.exp(jnp.maximum(rel, 0.0)[None] * lg[:, None, None]), 0.0)
    cdec = jnp.exp(lg[:, None] * (i + 1.0)[None, :])
    kdec = jnp.exp(lg[:, None] * (RET_CHUNK - 1.0 - i)[None, :])
    wide = lambda a: jnp.broadcast_to(a[:, :, None], (H_RET, RET_CHUNK, LANES))
    gc = jnp.broadcast_to(jnp.exp(lg * RET_CHUNK)[:, None, None], (H_RET, 1, LANES))
    g1 = jnp.broadcast_to(jnp.exp(lg)[:, None, None], (H_RET, 1, LANES))
    return (dmat, wide(cdec), wide(kdec), gc), g1


def _layer_weights(l, w_in, w_uq, w_ukv, w_o, w1, w3, w2):
    wi = w_in[l]
    w_r = wi[:, :R_COLS].astype(BF16)
    w_c = jnp.pad(wi[:, R_COLS:], ((0, 0), (0, LANES - QK_ROPE))).astype(BF16)
    uq = w_uq[l].reshape(Q_RANK, H_MLA, QK_NOPE + QK_ROPE)
    uq = jnp.pad(uq, ((0, 0), (0, 0), (0, HEAD_PAD - QK_NOPE - QK_ROPE))).reshape(Q_RANK, H_MLA * HEAD_PAD).astype(BF16)
    ukv = w_ukv[l].astype(BF16)
    ukv3 = w_ukv[l].reshape(KV_RANK, H_MLA, QK_NOPE + MLA_V)
    uk_t = jnp.transpose(ukv3[..., :QK_NOPE], (1, 2, 0)).astype(BF16)
    uv = jnp.transpose(ukv3[..., QK_NOPE:], (1, 0, 2)).astype(BF16)
    return dict(w_r=w_r, w_c=w_c, uq=uq, ukv=ukv, uk_t=uk_t, uv=uv, w_o=w_o[l].astype(BF16),
                w1=w1[l].astype(BF16), w3=w3[l].astype(BF16), w2=w2[l].astype(BF16))


def kernel(x_prompt, x_sample, cache_ckv, cache_kpe, state_ret, page_table, meta_tokens, ln0_g, ln0_b, w_in, q_norm_g, w_uq, kv_norm_g, w_ukv, ret_gn_g, ret_gn_b, w_o, ln1_g, ln1_b, w_router, router_bias, w1, w3, w2, ln2_g, ln2_b):
    b, seq, d = x_prompt.shape
    bd = x_sample.shape[0]
    t = seq + N_META
    past_len = page_table.shape[1] * PAGE_SIZE
    depth = w_in.shape[0]

    ret_tab_p = _rope_half_tables(jnp.arange(t))
    ret_tab_s = _rope_half_tables(jnp.full((1,), past_len))
    pad_tab_p = _rope_pad_tables(jnp.arange(t))
    pad_tab_s = _rope_pad_tables(jnp.full((bd,), past_len))
    ret_consts, gamma1 = _retention_consts()
    wr_hi = w_router.astype(BF16)
    wr_lo = (w_router - wr_hi.astype(F32)).astype(BF16)

    meta = jnp.broadcast_to(meta_tokens[None], (b, N_META, d))
    xp = jnp.concatenate([meta, x_prompt], 1).reshape(b * t, d)
    hp, hp_b = _ln(xp, ln0_g, ln0_b)
    hs, hs_b = _ln(x_sample.reshape(bd, d), ln0_g, ln0_b)

    outs = [[] for _ in range(6)]
    for l in range(depth):
        w = _layer_weights(l, w_in, w_uq, w_ukv, w_o, w1, w3, w2)
        qg, kg = q_norm_g[l].reshape(1, Q_RANK), kv_norm_g[l].reshape(1, KV_RANK)

        proj_r = _mm(hp_b, w["w_r"], 1024, "proj_r")
        proj_c = _mm(hp_b, w["w_c"], C_COLS + LANES, "proj_c")
        q, k, v, ckv, kpe = _mla_proj(proj_c, *pad_tab_p, qg, kg, w["uq"], w["ukv"])
        ret_o, ret_s = _retention_prompt(proj_r, *ret_tab_p, ret_consts, ret_gn_g[l], ret_gn_b[l], b, t)
        mla_o = _attention_prompt(q, k, v, b, t)
        h1, h1_b, gates = _oproj(ret_o.reshape(b * t, -1), mla_o.reshape(b * t, -1), hp, w["w_o"],
                                 ln1_g[l], ln1_b[l], wr_hi, wr_lo, router_bias)
        hp, hp_b = _res_ln(h1, _moe(h1_b, gates, w["w1"], w["w3"], w["w2"]), ln2_g[l], ln2_b[l])
        outs[0].append(ckv.reshape(b, t, KV_RANK))
        outs[1].append(kpe[:, :QK_ROPE].reshape(b, t, QK_ROPE))
        outs[2].append(ret_s)

        proj_r = _mm(hs_b, w["w_r"], 1024, "proj_r_s")
        proj_c = _mm(hs_b, w["w_c"], C_COLS + LANES, "proj_c_s")
        qlat, qpe, ckv, kpe = _sample_proj(proj_c, *pad_tab_s, qg, kg, w["uq"], w["uk_t"])
        ret_o, ret_s = _retention_step(l, proj_r, state_ret, *ret_tab_s, gamma1, ret_gn_g[l], ret_gn_b[l])
        o_lat = _decode_attention(l, page_table, cache_ckv, cache_kpe, qlat, qpe, ckv, kpe)
        mla_o = _uv_proj(o_lat, w["uv"])
        h1, h1_b, gates = _oproj(ret_o, mla_o, hs, w["w_o"], ln1_g[l], ln1_b[l], wr_hi, wr_lo, router_bias)
        hs, hs_b = _res_ln(h1, _moe(h1_b, gates, w["w1"], w["w3"], w["w2"]), ln2_g[l], ln2_b[l])
        outs[3].append(ckv.reshape(bd, 1, KV_RANK))
        outs[4].append(kpe[:, :QK_ROPE].reshape(bd, 1, QK_ROPE))
        outs[5].append(ret_s)

    y_prompt = hp.reshape(b, t, d)[:, N_META:]
    y_sample = hs.reshape(bd, 1, d)
    return (y_prompt, y_sample) + tuple(jnp.stack(o, 0) for o in outs)
```

```python
import functools
import math

import jax
import jax.numpy as jnp
from jax import lax
from jax.experimental import pallas as pl
from jax.experimental.pallas import tpu as pltpu

D_MODEL = 2048
N_META = 16
PAGE_SIZE = 128
H_RET = 8
RET_DK = 128
RET_DV = 128
RET_WIDTH = H_RET * RET_DV
RET_CHUNK = 128
H_MLA = 8
MLA_V = 128
MLA_WIDTH = H_MLA * MLA_V
QK_NOPE = 128
QK_ROPE = 64
Q_RANK = 512
KV_RANK = 512
SM_SCALE = (QK_NOPE + QK_ROPE) ** -0.5
N_EXPERTS = 16
N_GROUPS = 4
EXPERTS_PER_GROUP = N_EXPERTS // N_GROUPS
D_EXPERT = 512
ROPE_BASE = 10000.0
NORM_EPS = 1e-5
DEPTH = 2
ALPHA = (2 * DEPTH) ** 0.25
R_COLS = 2 * H_RET * RET_DK + 2 * RET_WIDTH
C_COLS = Q_RANK + KV_RANK
LANES = 128
HEAD_PAD = 2 * LANES
VMEM_LIMIT = 56 * 1024 * 1024
DECODE_PAGES = 16
RET_HEADS_PER_STEP = 2

BF16 = jnp.bfloat16
F32 = jnp.float32


def _tile(n, cap, align=16):
    best = None
    for t in range(align, min(n, cap) + 1, align):
        if n % t == 0:
            best = t
    return best if best is not None else n


def _params(*sem):
    return pltpu.CompilerParams(dimension_semantics=sem, vmem_limit_bytes=VMEM_LIMIT)


def _dot(a, b):
    return jnp.dot(a, b, preferred_element_type=F32)


def _dot_nt(a, b):
    return lax.dot_general(a, b, (((1,), (1,)), ((), ())), preferred_element_type=F32)


def _dot_tn(a, b):
    return lax.dot_general(a, b, (((0,), (0,)), ((), ())), preferred_element_type=F32)


def _split(x):
    hi = x.astype(BF16)
    return hi, (x - hi.astype(F32)).astype(BF16)


def _dot3(a, b, dot=_dot):
    ah, al = _split(a)
    bh, bl = _split(b)
    return dot(ah, bh) + (dot(al, bh) + dot(ah, bl))


def _layer_norm(x, g, b):
    mu = jnp.mean(x, -1, keepdims=True)
    xc = x - mu
    var = jnp.mean(xc * xc, -1, keepdims=True)
    return xc * lax.rsqrt(var + NORM_EPS) * g + b


def _rms_norm(x, g):
    return x * lax.rsqrt(jnp.mean(x * x, -1, keepdims=True) + NORM_EPS) * g


def _silu(x):
    return x * (1.0 / (1.0 + jnp.exp(-x)))


def _rope_half(x, c, s):
    return x * c + pltpu.roll(x, LANES // 2, 1) * s


def _rope_pad(u, c, a, b):
    return u * c + pltpu.roll(u, LANES - QK_ROPE // 2, 1) * a + pltpu.roll(u, QK_ROPE // 2, 1) * b


def _ln_kernel(x_ref, g_ref, b_ref, o_ref, ob_ref):
    y = _layer_norm(x_ref[...], g_ref[...], b_ref[...])
    o_ref[...] = y
    ob_ref[...] = y.astype(BF16)


def _ln(x, g, b):
    n, d = x.shape
    tm = _tile(n, 1024)
    row = pl.BlockSpec((tm, d), lambda i: (i, 0))
    vec = pl.BlockSpec((1, d), lambda i: (0, 0))
    return pl.pallas_call(
        _ln_kernel, grid=(n // tm,), in_specs=[row, vec, vec], out_specs=[row, row],
        out_shape=[jax.ShapeDtypeStruct((n, d), F32), jax.ShapeDtypeStruct((n, d), BF16)],
        compiler_params=_params("parallel"), name="ln0")(x, g.reshape(1, d), b.reshape(1, d))


def _mm_kernel(x_ref, w_ref, o_ref, *, precise):
    o_ref[...] = (_dot3 if precise else _dot)(x_ref[...], w_ref[...])


def _mm(x, w, tn, name, precise=False, layer=None, ncols=None):
    n, k = x.shape
    nw = ncols if ncols is not None else w.shape[-1]
    tm = _tile(n, 1024)
    if layer is None:
        w_spec = pl.BlockSpec((k, tn), lambda j, i: (0, j))
    else:
        w_spec = pl.BlockSpec((None, k, tn), lambda j, i: (layer, 0, j))
    return pl.pallas_call(
        functools.partial(_mm_kernel, precise=precise), grid=(nw // tn, n // tm),
        in_specs=[pl.BlockSpec((tm, k), lambda j, i: (i, 0)), w_spec],
        out_specs=pl.BlockSpec((tm, tn), lambda j, i: (i, j)),
        out_shape=jax.ShapeDtypeStruct((n, nw), F32),
        compiler_params=_params("parallel", "parallel"), name=name)(x, w)


def _mla_proj_kernel(x_ref, rc_ref, ra_ref, rb_ref, qg_ref, kg_ref, wuq_ref, wukv_ref,
                     q_ref, k_ref, v_ref, ckv_ref, kpe_ref):
    x = x_ref[...]
    rc, ra, rb = rc_ref[...], ra_ref[...], rb_ref[...]
    cq = _rms_norm(x[:, :Q_RANK], qg_ref[...]).astype(BF16)
    ckv = _rms_norm(x[:, Q_RANK:C_COLS], kg_ref[...])
    kpe = _rope_pad(x[:, C_COLS:], rc, ra, rb)
    ckv_ref[...] = ckv
    kpe_ref[...] = kpe
    kpe_b = kpe.astype(BF16)
    q = _dot(cq, wuq_ref[...])
    kv = _dot(ckv.astype(BF16), wukv_ref[...])
    for h in range(H_MLA):
        lo = h * HEAD_PAD
        q_ref[:, lo:lo + LANES] = q[:, lo:lo + LANES].astype(BF16)
        q_ref[:, lo + LANES:lo + HEAD_PAD] = _rope_pad(q[:, lo + LANES:lo + HEAD_PAD], rc, ra, rb).astype(BF16)
        k_ref[:, lo:lo + LANES] = kv[:, lo:lo + LANES].astype(BF16)
        k_ref[:, lo + LANES:lo + HEAD_PAD] = kpe_b
        v_ref[:, h * MLA_V:(h + 1) * MLA_V] = kv[:, lo + LANES:lo + HEAD_PAD].astype(BF16)


def _mla_proj(proj_c, rope_c, rope_a, rope_b, qg, kg, wuq, wukv):
    n, t = proj_c.shape[0], rope_c.shape[0]
    tm = _tile(t, 1024)
    row = lambda w: pl.BlockSpec((tm, w), lambda i: (i, 0))
    tab = pl.BlockSpec((tm, LANES), lambda i: (i % (t // tm), 0))
    full = lambda a: pl.BlockSpec(a.shape, lambda i: (0,) * a.ndim)
    return pl.pallas_call(
        _mla_proj_kernel, grid=(n // tm,),
        in_specs=[row(C_COLS + LANES), tab, tab, tab, full(qg), full(kg), full(wuq), full(wukv)],
        out_specs=[row(H_MLA * HEAD_PAD), row(H_MLA * HEAD_PAD), row(MLA_WIDTH), row(KV_RANK), row(LANES)],
        out_shape=[jax.ShapeDtypeStruct((n, H_MLA * HEAD_PAD), BF16), jax.ShapeDtypeStruct((n, H_MLA * HEAD_PAD), BF16),
                   jax.ShapeDtypeStruct((n, MLA_WIDTH), BF16), jax.ShapeDtypeStruct((n, KV_RANK), F32),
                   jax.ShapeDtypeStruct((n, LANES), F32)],
        compiler_params=_params("parallel"), name="mla_proj")(proj_c, rope_c, rope_a, rope_b, qg, kg, wuq, wukv)


def _ret_chunk(s, q, k, v, dmat, cdec, kdec, gc):
    qb, vb = q.astype(BF16), v.astype(BF16)
    inner = _dot((_dot_nt(qb, k.astype(BF16)) * dmat).astype(BF16), vb)
    cross = _dot(qb, s.astype(BF16)) * cdec
    s_new = gc * s + _dot_tn((k * kdec).astype(BF16), vb)
    return inner + cross, s_new


def _ret_out(o, g, gn_g, gn_b, dtype=BF16):
    mu = jnp.mean(o, -1, keepdims=True)
    oc = o - mu
    var = jnp.mean(oc * oc, -1, keepdims=True)
    on = oc * lax.rsqrt(var + NORM_EPS) * gn_g + gn_b
    return (_silu(g) * on).astype(dtype)


def _ret_kernel(q_ref, k_ref, v_ref, g_ref, c_ref, s_ref, dmat_ref, cdec_ref, kdec_ref, gc_ref, gng_ref, gnb_ref,
                o_ref, st_ref, pad_ref, *, n_chunks):
    kscale = RET_DK ** -0.5
    heads = range(RET_HEADS_PER_STEP)
    consts = [(dmat_ref[h], cdec_ref[h], kdec_ref[h], gc_ref[h]) for h in heads]

    def load(h, r0, rows):
        sl, ln = pl.ds(r0, rows), slice(h * LANES, (h + 1) * LANES)
        c, s = c_ref[sl, :], s_ref[sl, :]
        return (_rope_half(q_ref[sl, ln], c, s), _rope_half(k_ref[sl, ln], c, s) * kscale, v_ref[sl, ln], g_ref[sl, ln])

    lead = RET_CHUNK - N_META
    pad_ref[...] = jnp.zeros(pad_ref.shape, F32)
    states = []
    for h in heads:
        q0, k0, v0, g0 = load(h, 0, N_META)
        pad_ref[h, 0, lead:, :] = q0
        pad_ref[h, 1, lead:, :] = k0
        pad_ref[h, 2, lead:, :] = v0
        o0, s1 = _ret_chunk(jnp.zeros((RET_DK, RET_DV), F32), pad_ref[h, 0], pad_ref[h, 1], pad_ref[h, 2], *consts[h])
        o_ref[0:N_META, h * LANES:(h + 1) * LANES] = _ret_out(o0[lead:, :], g0, gng_ref[h], gnb_ref[h])
        states.append(s1)

    def body(c, states):
        r0 = pl.multiple_of(N_META + c * RET_CHUNK, 16)
        new = []
        for h in heads:
            q, k, v, g = load(h, r0, RET_CHUNK)
            o, s = _ret_chunk(states[h], q, k, v, *consts[h])
            o_ref[pl.ds(r0, RET_CHUNK), h * LANES:(h + 1) * LANES] = _ret_out(o, g, gng_ref[h], gnb_ref[h])
            new.append(s)
        return tuple(new)

    states = lax.fori_loop(0, n_chunks, body, tuple(states))
    for h in heads:
        st_ref[h] = states[h]


def _retention_prompt(proj_r, cos, sin, consts, gn_g, gn_b, b, t):
    dmat, cdec, kdec, gc = consts
    x = proj_r.reshape(b, t, R_COLS)
    hs, groups = RET_HEADS_PER_STEP, H_RET // RET_HEADS_PER_STEP
    head = lambda part: pl.BlockSpec((None, t, hs * LANES), lambda i, h: (i, 0, part * groups + h))
    tab = pl.BlockSpec((t, LANES), lambda i, h: (0, 0))
    per_h = lambda r: pl.BlockSpec((hs, r, LANES), lambda i, h: (h, 0, 0))
    return pl.pallas_call(
        functools.partial(_ret_kernel, n_chunks=(t - N_META) // RET_CHUNK), grid=(b, groups),
        in_specs=[head(0), head(1), head(2), head(3), tab, tab,
                  per_h(RET_CHUNK), per_h(RET_CHUNK), per_h(RET_CHUNK), per_h(1), per_h(1), per_h(1)],
        out_specs=[pl.BlockSpec((None, t, hs * LANES), lambda i, h: (i, 0, h)),
                   pl.BlockSpec((None, hs, RET_DK, RET_DV), lambda i, h: (i, h, 0, 0))],
        out_shape=[jax.ShapeDtypeStruct((b, t, RET_WIDTH), BF16), jax.ShapeDtypeStruct((b, H_RET, RET_DK, RET_DV), F32)],
        scratch_shapes=[pltpu.VMEM((hs, 3, RET_CHUNK, LANES), F32)],
        compiler_params=_params("parallel", "parallel"), name="retention_prompt",
    )(x, x, x, x, cos, sin, dmat, cdec, kdec, gc, gn_g.reshape(H_RET, 1, RET_DV), gn_b.reshape(H_RET, 1, RET_DV))


def _attn_kernel(q_ref, k_ref, v_ref, o_ref, *, t, blk):
    n_full = t // blk
    starts = [(i * blk, blk) for i in range(n_full)]
    if t % blk:
        starts.append((n_full * blk, t % blk))
    for qi, (q0, qn) in enumerate(starts):
        q = q_ref[q0:q0 + qn, :]
        m = jnp.full((qn, 1), -jnp.inf, F32)
        l = jnp.zeros((qn, 1), F32)
        acc = jnp.zeros((qn, MLA_V), F32)
        for ki, (k0, kn) in enumerate(starts[:qi + 1]):
            s = _dot_nt(q, k_ref[k0:k0 + kn, :]) * SM_SCALE
            if ki == qi:
                rows = lax.broadcasted_iota(jnp.int32, (qn, kn), 0)
                cols = lax.broadcasted_iota(jnp.int32, (qn, kn), 1)
                s = jnp.where(cols <= rows, s, -jnp.inf)
            m_new = jnp.maximum(m, jnp.max(s, -1, keepdims=True))
            p = jnp.exp(s - m_new)
            a = jnp.exp(m - m_new)
            l = a * l + jnp.sum(p, -1, keepdims=True)
            acc = a * acc + _dot(p.astype(BF16), v_ref[k0:k0 + kn, :])
            m = m_new
        o_ref[q0:q0 + qn, :] = (acc / l).astype(BF16)


def _attention_prompt(q, k, v, b, t):
    blk = 512 if t >= 512 else 128
    spec = lambda w: pl.BlockSpec((None, t, w), lambda i, h: (i, 0, h))
    return pl.pallas_call(
        functools.partial(_attn_kernel, t=t, blk=blk), grid=(b, H_MLA),
        in_specs=[spec(HEAD_PAD), spec(HEAD_PAD), spec(MLA_V)], out_specs=spec(MLA_V),
        out_shape=jax.ShapeDtypeStruct((b, t, MLA_WIDTH), BF16),
        compiler_params=_params("parallel", "parallel"), name="attention_prompt",
    )(q.reshape(b, t, -1), k.reshape(b, t, -1), v.reshape(b, t, -1))


def _route(h, wr_hi, wr_lo, bias):
    hi = h.astype(BF16)
    lo = (h - hi.astype(F32)).astype(BF16)
    logits = _dot(hi, wr_hi) + (_dot(lo, wr_hi) + _dot(hi, wr_lo))
    scores = 1.0 / (1.0 + jnp.exp(-logits))
    sel = scores + bias
    lane = lax.broadcasted_iota(jnp.int32, sel.shape, 1).astype(F32)
    neg = -jnp.inf

    def top2(vals):
        t1 = jnp.max(vals, -1, keepdims=True)
        i1 = jnp.min(jnp.where(vals == t1, lane, float(N_EXPERTS)), -1, keepdims=True)
        rest = jnp.where(lane == i1, neg, vals)
        t2 = jnp.max(rest, -1, keepdims=True)
        i2 = jnp.min(jnp.where(rest == t2, lane, float(N_EXPERTS)), -1, keepdims=True)
        return t1, i1, t2, i2

    def in_group(g):
        lo = g * float(EXPERTS_PER_GROUP)
        return jnp.where(lane >= lo, lane, float(N_EXPERTS)) < lo + EXPERTS_PER_GROUP

    best_score = best = None
    for g in range(N_GROUPS):
        t1, _, t2, _ = top2(jnp.where(in_group(float(g)), sel, neg))
        gs = t1 + t2
        if g == 0:
            best_score, best = gs, jnp.zeros_like(gs)
        else:
            upd = gs > best_score
            best_score = jnp.where(upd, gs, best_score)
            best = jnp.where(upd, float(g), best)
    in_best = in_group(best)
    _, i1, _, i2 = top2(jnp.where(in_best, sel, neg))
    s1 = jnp.sum(jnp.where(lane == i1, scores, 0.0), -1, keepdims=True)
    s2 = jnp.sum(jnp.where(lane == i2, scores, 0.0), -1, keepdims=True)
    tot = s1 + s2
    return jnp.where(lane == i1, s1 / tot, 0.0) + jnp.where(lane == i2, s2 / tot, 0.0)


def _oproj_kernel(r_ref, a_ref, h_ref, wa_ref, wb_ref, g_ref, b_ref, whi_ref, wlo_ref, rb_ref,
                  o_ref, ob_ref, gate_ref, *, precise):
    dot = _dot3 if precise else _dot
    y = dot(r_ref[...], wa_ref[...]) + dot(a_ref[...], wb_ref[...])
    h1 = _layer_norm(ALPHA * h_ref[...] + y, g_ref[...], b_ref[...])
    o_ref[...] = h1
    ob_ref[...] = h1.astype(BF16)
    gate_ref[...] = _route(h1, whi_ref[...], wlo_ref[...], rb_ref[...])


def _oproj(ret_o, mla_o, h, w_o, g, b, wr_hi, wr_lo, rbias, layer=None):
    n, d = h.shape
    tm = _tile(n, 512)
    half = w_o.shape[-2] // 2
    row = lambda w: pl.BlockSpec((tm, w), lambda i: (i, 0))
    vec = lambda w: pl.BlockSpec((1, w), lambda i: (0, 0))
    if layer is None:
        wspec = lambda j: pl.BlockSpec((half, d), lambda i: (j, 0))
    else:
        wspec = lambda j: pl.BlockSpec((None, half, d), lambda i: (layer, j, 0))
    rspec = pl.BlockSpec((d, N_EXPERTS), lambda i: (0, 0))
    return pl.pallas_call(
        functools.partial(_oproj_kernel, precise=layer is not None), grid=(n // tm,),
        in_specs=[row(half), row(half), row(d), wspec(0), wspec(1), vec(d), vec(d), rspec, rspec, vec(N_EXPERTS)],
        out_specs=[row(d), row(d), row(N_EXPERTS)],
        out_shape=[jax.ShapeDtypeStruct((n, d), F32), jax.ShapeDtypeStruct((n, d), BF16),
                   jax.ShapeDtypeStruct((n, N_EXPERTS), F32)],
        compiler_params=_params("parallel"), name="oproj_ln1_router",
    )(ret_o, mla_o, h, w_o, w_o, g.reshape(1, d), b.reshape(1, d), wr_hi, wr_lo, rbias.reshape(1, N_EXPERTS))


def _moe_kernel(x_ref, gate_ref, w1_ref, w3_ref, w2_ref, o_ref, acc_ref, *, precise):
    e = pl.program_id(1)

    @pl.when(e == 0)
    def _():
        acc_ref[...] = jnp.zeros(acc_ref.shape, F32)

    x = x_ref[...]
    gates = gate_ref[...]
    lane = lax.broadcasted_iota(jnp.int32, gates.shape, 1)
    ge = jnp.sum(jnp.where(lane == e, gates, 0.0), -1, keepdims=True)
    if precise:
        hid = _silu(_dot3(x, w1_ref[...])) * _dot3(x, w3_ref[...])
        acc_ref[...] += _dot3(hid * ge, w2_ref[...])
    else:
        hid = _silu(_dot(x, w1_ref[...])) * _dot(x, w3_ref[...])
        acc_ref[...] += _dot((hid * ge).astype(BF16), w2_ref[...])

    @pl.when(e == N_EXPERTS - 1)
    def _():
        o_ref[...] = acc_ref[...]


def _moe(x, gates, w1, w3, w2, layer=None):
    n, d = x.shape
    tm = _tile(n, 1024)
    row = lambda w: pl.BlockSpec((tm, w), lambda i, e: (i, 0))
    if layer is None:
        wspec = lambda r, c: pl.BlockSpec((None, r, c), lambda i, e: (e, 0, 0))
    else:
        wspec = lambda r, c: pl.BlockSpec((None, None, r, c), lambda i, e: (layer, e, 0, 0))
    return pl.pallas_call(
        functools.partial(_moe_kernel, precise=layer is not None), grid=(n // tm, N_EXPERTS),
        in_specs=[row(d), row(N_EXPERTS), wspec(d, D_EXPERT), wspec(d, D_EXPERT), wspec(D_EXPERT, d)],
        out_specs=row(d), out_shape=jax.ShapeDtypeStruct((n, d), F32),
        scratch_shapes=[pltpu.VMEM((tm, d), F32)],
        compiler_params=_params("parallel", "arbitrary"), name="moe")(x, gates, w1, w3, w2)


def _res_ln_kernel(h_ref, y_ref, g_ref, b_ref, o_ref, ob_ref):
    z = _layer_norm(ALPHA * h_ref[...] + y_ref[...], g_ref[...], b_ref[...])
    o_ref[...] = z
    ob_ref[...] = z.astype(BF16)


def _res_ln(h, y, g, b):
    n, d = h.shape
    tm = _tile(n, 1024)
    row = pl.BlockSpec((tm, d), lambda i: (i, 0))
    vec = pl.BlockSpec((1, d), lambda i: (0, 0))
    return pl.pallas_call(
        _res_ln_kernel, grid=(n // tm,), in_specs=[row, row, vec, vec], out_specs=[row, row],
        out_shape=[jax.ShapeDtypeStruct((n, d), F32), jax.ShapeDtypeStruct((n, d), BF16)],
        compiler_params=_params("parallel"), name="residual_ln2")(h, y, g.reshape(1, d), b.reshape(1, d))


def _sample_proj_kernel(x_ref, rc_ref, ra_ref, rb_ref, qg_ref, kg_ref, wuq_ref, wuk_ref,
                        qlat_ref, qpe_ref, ckv_ref, kpe_ref):
    x = x_ref[...]
    rc, ra, rb = rc_ref[...], ra_ref[...], rb_ref[...]
    cq = _rms_norm(x[:, :Q_RANK], qg_ref[...])
    ckv_ref[...] = _rms_norm(x[:, Q_RANK:C_COLS], kg_ref[...])
    kpe_ref[...] = _rope_pad(x[:, C_COLS:], rc, ra, rb)
    q = _dot3(cq, wuq_ref[...])
    for h in range(H_MLA):
        lo = h * HEAD_PAD
        qlat_ref[:, h * KV_RANK:(h + 1) * KV_RANK] = _dot3(q[:, lo:lo + LANES], wuk_ref[h]).astype(BF16)
        qpe_ref[:, h * LANES:(h + 1) * LANES] = _rope_pad(q[:, lo + LANES:lo + HEAD_PAD], rc, ra, rb).astype(BF16)


def _sample_proj(proj_c, rope_c, rope_a, rope_b, qg, kg, wuq, wuk_t):
    n = proj_c.shape[0]
    return pl.pallas_call(
        _sample_proj_kernel,
        out_shape=[jax.ShapeDtypeStruct((n, H_MLA * KV_RANK), BF16), jax.ShapeDtypeStruct((n, H_MLA * LANES), BF16),
                   jax.ShapeDtypeStruct((n, KV_RANK), F32), jax.ShapeDtypeStruct((n, LANES), F32)],
        compiler_params=pltpu.CompilerParams(vmem_limit_bytes=VMEM_LIMIT), name="sample_proj",
    )(proj_c, rope_c, rope_a, rope_b, qg, kg, wuq, wuk_t)


def _ret_step_kernel(x_ref, s_ref, c_ref, sn_ref, gc_ref, gng_ref, gnb_ref, o_ref, st_ref):
    x = x_ref[...]
    c, sn = c_ref[...], sn_ref[...]
    kscale = RET_DK ** -0.5
    w = H_RET * RET_DK
    rows = 16
    row = lax.broadcasted_iota(jnp.int32, (rows, RET_DK), 0)
    for h in range(H_RET):
        q = _rope_half(x[:, h * RET_DK:(h + 1) * RET_DK], c, sn)
        k = _rope_half(x[:, w + h * RET_DK:w + (h + 1) * RET_DK], c, sn) * kscale
        v = x[:, 2 * w + h * RET_DV:2 * w + (h + 1) * RET_DV]
        g = x[:, 3 * w + h * RET_DV:3 * w + (h + 1) * RET_DV]
        k0 = jnp.where(row == 0, k, 0.0)
        s_new = gc_ref[h] * s_ref[h] + _dot3(k0, jnp.broadcast_to(v, (rows, RET_DV)), _dot_tn)
        st_ref[h] = s_new
        o = _dot3(jnp.broadcast_to(q, (rows, RET_DK)), s_new)[:1, :]
        o_ref[:, h * RET_DV:(h + 1) * RET_DV] = _ret_out(o, g, gng_ref[h], gnb_ref[h], F32)


def _retention_step(layer, proj_r, state, cos, sin, gc, gn_g, gn_b):
    bd = proj_r.shape[0]
    full = lambda a: pl.BlockSpec(a.shape, lambda i: (0,) * a.ndim)
    gng, gnb = gn_g.reshape(H_RET, 1, RET_DV), gn_b.reshape(H_RET, 1, RET_DV)
    o, st = pl.pallas_call(
        _ret_step_kernel, grid=(bd,),
        in_specs=[pl.BlockSpec((None, 1, R_COLS), lambda i: (i, 0, 0)),
                  pl.BlockSpec((None, None, H_RET, RET_DK, RET_DV), lambda i: (layer, i, 0, 0, 0)),
                  full(cos), full(sin), full(gc), full(gng), full(gnb)],
        out_specs=[pl.BlockSpec((None, 1, RET_WIDTH), lambda i: (i, 0, 0)),
                   pl.BlockSpec((None, H_RET, RET_DK, RET_DV), lambda i: (i, 0, 0, 0))],
        out_shape=[jax.ShapeDtypeStruct((bd, 1, RET_WIDTH), F32), jax.ShapeDtypeStruct(state.shape[1:], F32)],
        compiler_params=_params("parallel"), name="retention_step",
    )(proj_r.reshape(bd, 1, R_COLS), state, cos, sin, gc, gng, gnb)
    return o.reshape(bd, RET_WIDTH), st


def _decode_kernel(pt_ref, *refs, pages):
    ckv_refs, kpe_refs = refs[:pages], refs[pages:2 * pages]
    qlat_ref, qpe_ref, cnew_ref, knew_ref, o_ref, m_ref, l_ref, acc_ref, cbuf_ref, pbuf_ref = refs[2 * pages:]
    j = pl.program_id(1)

    @pl.when(j == 0)
    def _():
        m_ref[...] = jnp.full(m_ref.shape, -jnp.inf, F32)
        l_ref[...] = jnp.zeros(l_ref.shape, F32)
        acc_ref[...] = jnp.zeros(acc_ref.shape, F32)

    qlat = qlat_ref[...]
    qpe = qpe_ref[...][:, :QK_ROPE]

    for p, (c_ref, k_ref) in enumerate(zip(ckv_refs, kpe_refs)):
        cbuf_ref[p * PAGE_SIZE:(p + 1) * PAGE_SIZE, :] = c_ref[...].astype(BF16)
        pbuf_ref[:, p * PAGE_SIZE:(p + 1) * PAGE_SIZE] = k_ref[...].astype(BF16)
    ckv = cbuf_ref[...]
    s = (_dot_nt(qlat, ckv) + _dot(qpe, pbuf_ref[...])) * SM_SCALE
    m_old = m_ref[...]
    m_new = jnp.maximum(m_old, jnp.max(s, -1, keepdims=True))
    p = jnp.exp(s - m_new)
    a = jnp.exp(m_old - m_new)
    l_ref[...] = a * l_ref[...] + jnp.sum(p, -1, keepdims=True)
    acc_ref[...] = a * acc_ref[...] + _dot(p.astype(BF16), ckv)
    m_ref[...] = m_new

    @pl.when(j == pl.num_programs(1) - 1)
    def _():
        cn = jnp.broadcast_to(cnew_ref[...], (8, KV_RANK)).astype(BF16)
        kn = jnp.broadcast_to(knew_ref[...][:, :QK_ROPE], (8, QK_ROPE)).astype(BF16)
        s = (_dot_nt(qlat, cn) + _dot_nt(qpe, kn))[:, :1] * SM_SCALE
        m_old = m_ref[...]
        m_new = jnp.maximum(m_old, s)
        p = jnp.exp(s - m_new)
        a = jnp.exp(m_old - m_new)
        l = a * l_ref[...] + p
        acc = a * acc_ref[...] + p.astype(BF16).astype(F32) * cn[:1, :].astype(F32)
        o = acc / l
        for h in range(H_MLA):
            o_ref[:, h * KV_RANK:(h + 1) * KV_RANK] = o[h:h + 1, :]


def _decode_attention(layer, page_table, cache_ckv, cache_kpe_t, qlat, qpe, ckv_new, kpe_new):
    bd, n_pages = page_table.shape
    pages = _tile(n_pages, DECODE_PAGES, 1)
    ckv_spec = lambda p: pl.BlockSpec((None, None, PAGE_SIZE, KV_RANK),
                                      lambda i, j, pt: (layer, pt[i, j * pages + p], 0, 0))
    kpe_spec = lambda p: pl.BlockSpec((None, None, QK_ROPE, PAGE_SIZE),
                                      lambda i, j, pt: (layer, pt[i, j * pages + p], 0, 0))
    per_b = lambda r, w: pl.BlockSpec((None, r, w), lambda i, j, pt: (i, 0, 0))
    grid_spec = pltpu.PrefetchScalarGridSpec(
        num_scalar_prefetch=1, grid=(bd, n_pages // pages),
        in_specs=[ckv_spec(p) for p in range(pages)] + [kpe_spec(p) for p in range(pages)]
        + [per_b(H_MLA, KV_RANK), per_b(H_MLA, LANES), per_b(1, KV_RANK), per_b(1, LANES)],
        out_specs=per_b(1, H_MLA * KV_RANK),
        scratch_shapes=[pltpu.VMEM((H_MLA, 1), F32), pltpu.VMEM((H_MLA, 1), F32), pltpu.VMEM((H_MLA, KV_RANK), F32),
                        pltpu.VMEM((pages * PAGE_SIZE, KV_RANK), BF16), pltpu.VMEM((QK_ROPE, pages * PAGE_SIZE), BF16)])
    o = pl.pallas_call(
        functools.partial(_decode_kernel, pages=pages), grid_spec=grid_spec,
        out_shape=jax.ShapeDtypeStruct((bd, 1, H_MLA * KV_RANK), F32),
        compiler_params=_params("parallel", "arbitrary"), name="decode_attention",
    )(page_table, *([cache_ckv] * pages), *([cache_kpe_t] * pages),
      qlat.reshape(bd, H_MLA, KV_RANK), qpe.reshape(bd, H_MLA, LANES),
      ckv_new.reshape(bd, 1, KV_RANK), kpe_new.reshape(bd, 1, LANES))
    return o.reshape(bd, H_MLA * KV_RANK)


def _uv_kernel(o_ref, w_ref, out_ref):
    for h in range(H_MLA):
        out_ref[:, h * MLA_V:(h + 1) * MLA_V] = _dot3(o_ref[:, h * KV_RANK:(h + 1) * KV_RANK], w_ref[h])


def _uv_proj(o_lat, w_uv):
    bd = o_lat.shape[0]
    return pl.pallas_call(
        _uv_kernel, out_shape=jax.ShapeDtypeStruct((bd, MLA_WIDTH), F32),
        compiler_params=pltpu.CompilerParams(vmem_limit_bytes=VMEM_LIMIT), name="uv_proj")(o_lat, w_uv)


def _rope_tables(pos, half):
    inv = ROPE_BASE ** (-jnp.arange(half, dtype=F32) / half)
    ang = pos.astype(F32)[:, None] * inv[None, :]
    return jnp.cos(ang), jnp.sin(ang)


def _rope_half_tables(pos):
    cos, sin = _rope_tables(pos, RET_DK // 2)
    return jnp.concatenate([cos, cos], -1), jnp.concatenate([-sin, sin], -1)


def _rope_pad_tables(pos):
    cos, sin = _rope_tables(pos, QK_ROPE // 2)
    z = jnp.zeros_like(cos)
    zz = jnp.zeros((pos.shape[0], LANES - QK_ROPE), F32)
    return (jnp.concatenate([cos, cos, zz], -1), jnp.concatenate([-sin, z, zz], -1),
            jnp.concatenate([z, sin, zz], -1))


def _retention_consts():
    lg = jnp.log1p(-jnp.exp2(-5.0 - jnp.arange(H_RET, dtype=F32)))
    i = jnp.arange(RET_CHUNK, dtype=F32)
    rel = i[:, None] - i[None, :]
    dmat = jnp.where(rel[None] >= 0, jnp.exp(jnp.maximum(rel, 0.0)[None] * lg[:, None, None]), 0.0)
    cdec = jnp.exp(lg[:, None] * (i + 1.0)[None, :])
    kdec = jnp.exp(lg[:, None] * (RET_CHUNK - 1.0 - i)[None, :])
    wide = lambda a: jnp.broadcast_to(a[:, :, None], (H_RET, RET_CHUNK, LANES))
    gc = jnp.broadcast_to(jnp.exp(lg * RET_CHUNK)[:, None, None], (H_RET, 1, LANES))
    g1 = jnp.broadcast_to(jnp.exp(lg)[:, None, None], (H_RET, 1, LANES))
    return (dmat, wide(cdec), wide(kdec), gc), g1


def _layer_weights(l, w_in, w_uq, w_ukv, w_o, w1, w3, w2):
    wi = w_in[l]
    w_r = wi[:, :R_COLS].astype(BF16)
    w_c32 = jnp.pad(wi[:, R_COLS:], ((0, 0), (0, LANES - QK_ROPE)))
    uq = w_uq[l].reshape(Q_RANK, H_MLA, QK_NOPE + QK_ROPE)
    uq32 = jnp.pad(uq, ((0, 0), (0, 0), (0, HEAD_PAD - QK_NOPE - QK_ROPE))).reshape(Q_RANK, H_MLA * HEAD_PAD)
    ukv = w_ukv[l].astype(BF16)
    ukv3 = w_ukv[l].reshape(KV_RANK, H_MLA, QK_NOPE + MLA_V)
    uk_t32 = jnp.transpose(ukv3[..., :QK_NOPE], (1, 2, 0))
    uv32 = jnp.transpose(ukv3[..., QK_NOPE:], (1, 0, 2))
    return dict(w_r=w_r, w_c=w_c32.astype(BF16), w_c32=w_c32, uq=uq32.astype(BF16), uq32=uq32, ukv=ukv,
                uk_t32=uk_t32, uv32=uv32, w_o=w_o[l].astype(BF16),
                w1=w1[l].astype(BF16), w3=w3[l].astype(BF16), w2=w2[l].astype(BF16))


def kernel(x_prompt, x_sample, cache_ckv, cache_kpe, state_ret, page_table, meta_tokens, ln0_g, ln0_b, w_in, q_norm_g, w_uq, kv_norm_g, w_ukv, ret_gn_g, ret_gn_b, w_o, ln1_g, ln1_b, w_router, router_bias, w1, w3, w2, ln2_g, ln2_b):
    b, seq, d = x_prompt.shape
    bd = x_sample.shape[0]
    t = seq + N_META
    past_len = page_table.shape[1] * PAGE_SIZE
    depth = w_in.shape[0]

    ret_tab_p = _rope_half_tables(jnp.arange(t))
    ret_tab_s = _rope_half_tables(jnp.full((1,), past_len))
    pad_tab_p = _rope_pad_tables(jnp.arange(t))
    pad_tab_s = _rope_pad_tables(jnp.full((bd,), past_len))
    ret_consts, gamma1 = _retention_consts()
    cache_kpe_t = jnp.swapaxes(cache_kpe, 2, 3)
    wr_hi = w_router.astype(BF16)
    wr_lo = (w_router - wr_hi.astype(F32)).astype(BF16)

    meta = jnp.broadcast_to(meta_tokens[None], (b, N_META, d))
    xp = jnp.concatenate([meta, x_prompt], 1).reshape(b * t, d)
    hp, hp_b = _ln(xp, ln0_g, ln0_b)
    hs, hs_b = _ln(x_sample.reshape(bd, d), ln0_g, ln0_b)

    outs = [[] for _ in range(6)]
    for l in range(depth):
        w = _layer_weights(l, w_in, w_uq, w_ukv, w_o, w1, w3, w2)
        qg, kg = q_norm_g[l].reshape(1, Q_RANK), kv_norm_g[l].reshape(1, KV_RANK)

        proj_r = _mm(hp_b, w["w_r"], 1024, "proj_r")
        proj_c = _mm(hp_b, w["w_c"], C_COLS + LANES, "proj_c")
        q, k, v, ckv, kpe = _mla_proj(proj_c, *pad_tab_p, qg, kg, w["uq"], w["ukv"])
        ret_o, ret_s = _retention_prompt(proj_r, *ret_tab_p, ret_consts, ret_gn_g[l], ret_gn_b[l], b, t)
        mla_o = _attention_prompt(q, k, v, b, t)
        h1, h1_b, gates = _oproj(ret_o.reshape(b * t, -1), mla_o.reshape(b * t, -1), hp, w["w_o"],
                                 ln1_g[l], ln1_b[l], wr_hi, wr_lo, router_bias)
        hp, hp_b = _res_ln(h1, _moe(h1_b, gates, w["w1"], w["w3"], w["w2"]), ln2_g[l], ln2_b[l])
        outs[0].append(ckv.reshape(b, t, KV_RANK))
        outs[1].append(kpe[:, :QK_ROPE].reshape(b, t, QK_ROPE))
        outs[2].append(ret_s)

        proj_r = _mm(hs, w_in, 1024, "proj_r_s", precise=True, layer=l, ncols=R_COLS)
        proj_c = _mm(hs, w["w_c32"], C_COLS + LANES, "proj_c_s", precise=True)
        qlat, qpe, ckv, kpe = _sample_proj(proj_c, *pad_tab_s, qg, kg, w["uq32"], w["uk_t32"])
        ret_o, ret_s = _retention_step(l, proj_r, state_ret, *ret_tab_s, gamma1, ret_gn_g[l], ret_gn_b[l])
        o_lat = _decode_attention(l, page_table, cache_ckv, cache_kpe_t, qlat, qpe, ckv, kpe)
        mla_o = _uv_proj(o_lat, w["uv32"])
        h1, _, gates = _oproj(ret_o, mla_o, hs, w_o, ln1_g[l], ln1_b[l], wr_hi, wr_lo, router_bias, layer=l)
        hs, _ = _res_ln(h1, _moe(h1, gates, w1, w3, w2, layer=l), ln2_g[l], ln2_b[l])
        outs[3].append(ckv.reshape(bd, 1, KV_RANK))
        outs[4].append(kpe[:, :QK_ROPE].reshape(bd, 1, QK_ROPE))
        outs[5].append(ret_s)

    y_prompt = hp.reshape(b, t, d)[:, N_META:]
    y_sample = hs.reshape(bd, 1, d)
    return (y_prompt, y_sample) + tuple(jnp.stack(o, 0) for o in outs)
```

```python
import functools
import math

import jax
import jax.numpy as jnp
from jax import lax
from jax.experimental import pallas as pl
from jax.experimental.pallas import tpu as pltpu

D_MODEL = 2048
N_META = 16
PAGE_SIZE = 128
H_RET = 8
RET_DK = 128
RET_DV = 128
RET_WIDTH = H_RET * RET_DV
RET_CHUNK = 128
H_MLA = 8
MLA_V = 128
MLA_WIDTH = H_MLA * MLA_V
QK_NOPE = 128
QK_ROPE = 64
Q_RANK = 512
KV_RANK = 512
SM_SCALE = (QK_NOPE + QK_ROPE) ** -0.5
N_EXPERTS = 16
N_GROUPS = 4
EXPERTS_PER_GROUP = N_EXPERTS // N_GROUPS
D_EXPERT = 512
ROPE_BASE = 10000.0
NORM_EPS = 1e-5
DEPTH = 2
ALPHA = (2 * DEPTH) ** 0.25
R_COLS = 2 * H_RET * RET_DK + 2 * RET_WIDTH
C_COLS = Q_RANK + KV_RANK
LANES = 128
HEAD_PAD = 2 * LANES
VMEM_LIMIT = 56 * 1024 * 1024
DECODE_PAGES = 16
RET_HEADS_PER_STEP = 2
MOE_CHUNK = 128

BF16 = jnp.bfloat16
F32 = jnp.float32


def _tile(n, cap, align=16):
    best = None
    for t in range(align, min(n, cap) + 1, align):
        if n % t == 0:
            best = t
    return best if best is not None else n


def _params(*sem):
    return pltpu.CompilerParams(dimension_semantics=sem, vmem_limit_bytes=VMEM_LIMIT)


def _dot(a, b):
    return jnp.dot(a, b, preferred_element_type=F32)


def _dot_nt(a, b):
    return lax.dot_general(a, b, (((1,), (1,)), ((), ())), preferred_element_type=F32)


def _dot_tn(a, b):
    return lax.dot_general(a, b, (((0,), (0,)), ((), ())), preferred_element_type=F32)


def _split(x):
    hi = x.astype(BF16)
    return hi, (x - hi.astype(F32)).astype(BF16)


def _dot3(a, b, dot=_dot):
    ah, al = _split(a)
    bh, bl = _split(b)
    return dot(ah, bh) + (dot(al, bh) + dot(ah, bl))


def _layer_norm(x, g, b):
    mu = jnp.mean(x, -1, keepdims=True)
    xc = x - mu
    var = jnp.mean(xc * xc, -1, keepdims=True)
    return xc * lax.rsqrt(var + NORM_EPS) * g + b


def _rms_norm(x, g):
    return x * lax.rsqrt(jnp.mean(x * x, -1, keepdims=True) + NORM_EPS) * g


def _silu(x):
    return x * (1.0 / (1.0 + jnp.exp(-x)))


def _rope_half(x, c, s):
    return x * c + pltpu.roll(x, LANES // 2, 1) * s


def _rope_pad(u, c, a, b):
    return u * c + pltpu.roll(u, LANES - QK_ROPE // 2, 1) * a + pltpu.roll(u, QK_ROPE // 2, 1) * b


def _ln_kernel(x_ref, g_ref, b_ref, o_ref, ob_ref):
    y = _layer_norm(x_ref[...], g_ref[...], b_ref[...])
    o_ref[...] = y
    ob_ref[...] = y.astype(BF16)


def _ln(x, g, b):
    n, d = x.shape
    tm = _tile(n, 1024)
    row = pl.BlockSpec((tm, d), lambda i: (i, 0))
    vec = pl.BlockSpec((1, d), lambda i: (0, 0))
    return pl.pallas_call(
        _ln_kernel, grid=(n // tm,), in_specs=[row, vec, vec], out_specs=[row, row],
        out_shape=[jax.ShapeDtypeStruct((n, d), F32), jax.ShapeDtypeStruct((n, d), BF16)],
        compiler_params=_params("parallel"), name="ln0")(x, g.reshape(1, d), b.reshape(1, d))


def _mm_kernel(x_ref, w_ref, o_ref, *, precise):
    o_ref[...] = (_dot3 if precise else _dot)(x_ref[...], w_ref[...])


def _mm(x, w, tn, name, precise=False, layer=None, ncols=None):
    n, k = x.shape
    nw = ncols if ncols is not None else w.shape[-1]
    tm = _tile(n, 1024)
    if layer is None:
        w_spec = pl.BlockSpec((k, tn), lambda j, i: (0, j))
    else:
        w_spec = pl.BlockSpec((None, k, tn), lambda j, i: (layer, 0, j))
    return pl.pallas_call(
        functools.partial(_mm_kernel, precise=precise), grid=(nw // tn, n // tm),
        in_specs=[pl.BlockSpec((tm, k), lambda j, i: (i, 0)), w_spec],
        out_specs=pl.BlockSpec((tm, tn), lambda j, i: (i, j)),
        out_shape=jax.ShapeDtypeStruct((n, nw), F32),
        compiler_params=_params("parallel", "parallel"), name=name)(x, w)


def _mla_proj_kernel(x_ref, rc_ref, ra_ref, rb_ref, qg_ref, kg_ref, wuq_ref, wukv_ref,
                     q_ref, k_ref, v_ref, ckv_ref, kpe_ref):
    x = x_ref[...]
    rc, ra, rb = rc_ref[...], ra_ref[...], rb_ref[...]
    cq = _rms_norm(x[:, :Q_RANK], qg_ref[...]).astype(BF16)
    ckv = _rms_norm(x[:, Q_RANK:C_COLS], kg_ref[...])
    kpe = _rope_pad(x[:, C_COLS:], rc, ra, rb)
    ckv_ref[...] = ckv
    kpe_ref[...] = kpe
    kpe_b = kpe.astype(BF16)
    q = _dot(cq, wuq_ref[...])
    kv = _dot(ckv.astype(BF16), wukv_ref[...])
    for h in range(H_MLA):
        lo = h * HEAD_PAD
        q_ref[:, lo:lo + LANES] = q[:, lo:lo + LANES].astype(BF16)
        q_ref[:, lo + LANES:lo + HEAD_PAD] = _rope_pad(q[:, lo + LANES:lo + HEAD_PAD], rc, ra, rb).astype(BF16)
        k_ref[:, lo:lo + LANES] = kv[:, lo:lo + LANES].astype(BF16)
        k_ref[:, lo + LANES:lo + HEAD_PAD] = kpe_b
        v_ref[:, h * MLA_V:(h + 1) * MLA_V] = kv[:, lo + LANES:lo + HEAD_PAD].astype(BF16)


def _mla_proj(proj_c, rope_c, rope_a, rope_b, qg, kg, wuq, wukv):
    n, t = proj_c.shape[0], rope_c.shape[0]
    tm = _tile(t, 1024)
    row = lambda w: pl.BlockSpec((tm, w), lambda i: (i, 0))
    tab = pl.BlockSpec((tm, LANES), lambda i: (i % (t // tm), 0))
    full = lambda a: pl.BlockSpec(a.shape, lambda i: (0,) * a.ndim)
    return pl.pallas_call(
        _mla_proj_kernel, grid=(n // tm,),
        in_specs=[row(C_COLS + LANES), tab, tab, tab, full(qg), full(kg), full(wuq), full(wukv)],
        out_specs=[row(H_MLA * HEAD_PAD), row(H_MLA * HEAD_PAD), row(MLA_WIDTH), row(KV_RANK), row(LANES)],
        out_shape=[jax.ShapeDtypeStruct((n, H_MLA * HEAD_PAD), BF16), jax.ShapeDtypeStruct((n, H_MLA * HEAD_PAD), BF16),
                   jax.ShapeDtypeStruct((n, MLA_WIDTH), BF16), jax.ShapeDtypeStruct((n, KV_RANK), F32),
                   jax.ShapeDtypeStruct((n, LANES), F32)],
        compiler_params=_params("parallel"), name="mla_proj")(proj_c, rope_c, rope_a, rope_b, qg, kg, wuq, wukv)


def _ret_chunk(s, q, k, v, dmat, cdec, kdec, gc):
    qb, vb = q.astype(BF16), v.astype(BF16)
    inner = _dot((_dot_nt(qb, k.astype(BF16)) * dmat).astype(BF16), vb)
    cross = _dot(qb, s.astype(BF16)) * cdec
    s_new = gc * s + _dot_tn((k * kdec).astype(BF16), vb)
    return inner + cross, s_new


def _ret_out(o, g, gn_g, gn_b, dtype=BF16):
    mu = jnp.mean(o, -1, keepdims=True)
    oc = o - mu
    var = jnp.mean(oc * oc, -1, keepdims=True)
    on = oc * lax.rsqrt(var + NORM_EPS) * gn_g + gn_b
    return (_silu(g) * on).astype(dtype)


def _ret_kernel(q_ref, k_ref, v_ref, g_ref, c_ref, s_ref, dmat_ref, cdec_ref, kdec_ref, gc_ref, gng_ref, gnb_ref,
                o_ref, st_ref, pad_ref, *, n_chunks):
    kscale = RET_DK ** -0.5
    heads = range(RET_HEADS_PER_STEP)
    consts = [(dmat_ref[h], cdec_ref[h], kdec_ref[h], gc_ref[h]) for h in heads]

    def load(h, r0, rows):
        sl, ln = pl.ds(r0, rows), slice(h * LANES, (h + 1) * LANES)
        c, s = c_ref[sl, :], s_ref[sl, :]
        return (_rope_half(q_ref[sl, ln], c, s), _rope_half(k_ref[sl, ln], c, s) * kscale, v_ref[sl, ln], g_ref[sl, ln])

    lead = RET_CHUNK - N_META
    pad_ref[...] = jnp.zeros(pad_ref.shape, F32)
    states = []
    for h in heads:
        q0, k0, v0, g0 = load(h, 0, N_META)
        pad_ref[h, 0, lead:, :] = q0
        pad_ref[h, 1, lead:, :] = k0
        pad_ref[h, 2, lead:, :] = v0
        o0, s1 = _ret_chunk(jnp.zeros((RET_DK, RET_DV), F32), pad_ref[h, 0], pad_ref[h, 1], pad_ref[h, 2], *consts[h])
        o_ref[0:N_META, h * LANES:(h + 1) * LANES] = _ret_out(o0[lead:, :], g0, gng_ref[h], gnb_ref[h])
        states.append(s1)

    def body(c, states):
        r0 = pl.multiple_of(N_META + c * RET_CHUNK, 16)
        new = []
        for h in heads:
            q, k, v, g = load(h, r0, RET_CHUNK)
            o, s = _ret_chunk(states[h], q, k, v, *consts[h])
            o_ref[pl.ds(r0, RET_CHUNK), h * LANES:(h + 1) * LANES] = _ret_out(o, g, gng_ref[h], gnb_ref[h])
            new.append(s)
        return tuple(new)

    states = lax.fori_loop(0, n_chunks, body, tuple(states))
    for h in heads:
        st_ref[h] = states[h]


def _retention_prompt(proj_r, cos, sin, consts, gn_g, gn_b, b, t):
    dmat, cdec, kdec, gc = consts
    x = proj_r.reshape(b, t, R_COLS)
    hs, groups = RET_HEADS_PER_STEP, H_RET // RET_HEADS_PER_STEP
    head = lambda part: pl.BlockSpec((None, t, hs * LANES), lambda i, h: (i, 0, part * groups + h))
    tab = pl.BlockSpec((t, LANES), lambda i, h: (0, 0))
    per_h = lambda r: pl.BlockSpec((hs, r, LANES), lambda i, h: (h, 0, 0))
    return pl.pallas_call(
        functools.partial(_ret_kernel, n_chunks=(t - N_META) // RET_CHUNK), grid=(b, groups),
        in_specs=[head(0), head(1), head(2), head(3), tab, tab,
                  per_h(RET_CHUNK), per_h(RET_CHUNK), per_h(RET_CHUNK), per_h(1), per_h(1), per_h(1)],
        out_specs=[pl.BlockSpec((None, t, hs * LANES), lambda i, h: (i, 0, h)),
                   pl.BlockSpec((None, hs, RET_DK, RET_DV), lambda i, h: (i, h, 0, 0))],
        out_shape=[jax.ShapeDtypeStruct((b, t, RET_WIDTH), BF16), jax.ShapeDtypeStruct((b, H_RET, RET_DK, RET_DV), F32)],
        scratch_shapes=[pltpu.VMEM((hs, 3, RET_CHUNK, LANES), F32)],
        compiler_params=_params("parallel", "parallel"), name="retention_prompt",
    )(x, x, x, x, cos, sin, dmat, cdec, kdec, gc, gn_g.reshape(H_RET, 1, RET_DV), gn_b.reshape(H_RET, 1, RET_DV))


def _attn_kernel(q_ref, k_ref, v_ref, o_ref, *, t, blk):
    n_full = t // blk
    starts = [(i * blk, blk) for i in range(n_full)]
    if t % blk:
        starts.append((n_full * blk, t % blk))
    for qi, (q0, qn) in enumerate(starts):
        q = q_ref[q0:q0 + qn, :]
        m = jnp.full((qn, 1), -jnp.inf, F32)
        l = jnp.zeros((qn, 1), F32)
        acc = jnp.zeros((qn, MLA_V), F32)
        for ki, (k0, kn) in enumerate(starts[:qi + 1]):
            s = _dot_nt(q, k_ref[k0:k0 + kn, :]) * SM_SCALE
            if ki == qi:
                rows = lax.broadcasted_iota(jnp.int32, (qn, kn), 0)
                cols = lax.broadcasted_iota(jnp.int32, (qn, kn), 1)
                s = jnp.where(cols <= rows, s, -jnp.inf)
            m_new = jnp.maximum(m, jnp.max(s, -1, keepdims=True))
            p = jnp.exp(s - m_new)
            a = jnp.exp(m - m_new)
            l = a * l + jnp.sum(p, -1, keepdims=True)
            acc = a * acc + _dot(p.astype(BF16), v_ref[k0:k0 + kn, :])
            m = m_new
        o_ref[q0:q0 + qn, :] = (acc / l).astype(BF16)


def _attention_prompt(q, k, v, b, t):
    blk = 512 if t >= 512 else 128
    spec = lambda w: pl.BlockSpec((None, t, w), lambda i, h: (i, 0, h))
    return pl.pallas_call(
        functools.partial(_attn_kernel, t=t, blk=blk), grid=(b, H_MLA),
        in_specs=[spec(HEAD_PAD), spec(HEAD_PAD), spec(MLA_V)], out_specs=spec(MLA_V),
        out_shape=jax.ShapeDtypeStruct((b, t, MLA_WIDTH), BF16),
        compiler_params=_params("parallel", "parallel"), name="attention_prompt",
    )(q.reshape(b, t, -1), k.reshape(b, t, -1), v.reshape(b, t, -1))


def _route(h, wr_hi, wr_lo, bias):
    hi = h.astype(BF16)
    lo = (h - hi.astype(F32)).astype(BF16)
    logits = _dot(hi, wr_hi) + (_dot(lo, wr_hi) + _dot(hi, wr_lo))
    scores = 1.0 / (1.0 + jnp.exp(-logits))
    sel = scores + bias
    lane = lax.broadcasted_iota(jnp.int32, sel.shape, 1).astype(F32)
    neg = -jnp.inf

    def top2(vals):
        t1 = jnp.max(vals, -1, keepdims=True)
        i1 = jnp.min(jnp.where(vals == t1, lane, float(N_EXPERTS)), -1, keepdims=True)
        rest = jnp.where(lane == i1, neg, vals)
        t2 = jnp.max(rest, -1, keepdims=True)
        i2 = jnp.min(jnp.where(rest == t2, lane, float(N_EXPERTS)), -1, keepdims=True)
        return t1, i1, t2, i2

    def in_group(g):
        lo = g * float(EXPERTS_PER_GROUP)
        return jnp.where(lane >= lo, lane, float(N_EXPERTS)) < lo + EXPERTS_PER_GROUP

    best_score = best = None
    for g in range(N_GROUPS):
        t1, _, t2, _ = top2(jnp.where(in_group(float(g)), sel, neg))
        gs = t1 + t2
        if g == 0:
            best_score, best = gs, jnp.zeros_like(gs)
        else:
            upd = gs > best_score
            best_score = jnp.where(upd, gs, best_score)
            best = jnp.where(upd, float(g), best)
    in_best = in_group(best)
    _, i1, _, i2 = top2(jnp.where(in_best, sel, neg))
    s1 = jnp.sum(jnp.where(lane == i1, scores, 0.0), -1, keepdims=True)
    s2 = jnp.sum(jnp.where(lane == i2, scores, 0.0), -1, keepdims=True)
    tot = s1 + s2
    return jnp.where(lane == i1, s1 / tot, 0.0) + jnp.where(lane == i2, s2 / tot, 0.0)


def _oproj_kernel(r_ref, a_ref, h_ref, wa_ref, wb_ref, g_ref, b_ref, whi_ref, wlo_ref, rb_ref,
                  o_ref, ob_ref, gate_ref, *, precise):
    dot = _dot3 if precise else _dot
    y = dot(r_ref[...], wa_ref[...]) + dot(a_ref[...], wb_ref[...])
    h1 = _layer_norm(ALPHA * h_ref[...] + y, g_ref[...], b_ref[...])
    o_ref[...] = h1
    ob_ref[...] = h1.astype(BF16)
    gate_ref[...] = _route(h1, whi_ref[...], wlo_ref[...], rb_ref[...])


def _oproj(ret_o, mla_o, h, w_o, g, b, wr_hi, wr_lo, rbias, layer=None):
    n, d = h.shape
    tm = _tile(n, 512)
    half = w_o.shape[-2] // 2
    row = lambda w: pl.BlockSpec((tm, w), lambda i: (i, 0))
    vec = lambda w: pl.BlockSpec((1, w), lambda i: (0, 0))
    if layer is None:
        wspec = lambda j: pl.BlockSpec((half, d), lambda i: (j, 0))
    else:
        wspec = lambda j: pl.BlockSpec((None, half, d), lambda i: (layer, j, 0))
    rspec = pl.BlockSpec((d, N_EXPERTS), lambda i: (0, 0))
    return pl.pallas_call(
        functools.partial(_oproj_kernel, precise=layer is not None), grid=(n // tm,),
        in_specs=[row(half), row(half), row(d), wspec(0), wspec(1), vec(d), vec(d), rspec, rspec, vec(N_EXPERTS)],
        out_specs=[row(d), row(d), row(N_EXPERTS)],
        out_shape=[jax.ShapeDtypeStruct((n, d), F32), jax.ShapeDtypeStruct((n, d), BF16),
                   jax.ShapeDtypeStruct((n, N_EXPERTS), F32)],
        compiler_params=_params("parallel"), name="oproj_ln1_router",
    )(ret_o, mla_o, h, w_o, w_o, g.reshape(1, d), b.reshape(1, d), wr_hi, wr_lo, rbias.reshape(1, N_EXPERTS))


def _gate_column(gates, e):
    lane = lax.broadcasted_iota(jnp.int32, gates.shape, 1)
    return jnp.sum(jnp.where(lane == e, gates, 0.0), -1, keepdims=True)


def _moe_dense_kernel(x_ref, gate_ref, w1_ref, w3_ref, w2_ref, o_ref, acc_ref):
    e = pl.program_id(1)

    @pl.when(e == 0)
    def _():
        acc_ref[...] = jnp.zeros(acc_ref.shape, F32)

    x = x_ref[...]
    hid = _silu(_dot3(x, w1_ref[...])) * _dot3(x, w3_ref[...])
    acc_ref[...] += _dot3(hid * _gate_column(gate_ref[...], e), w2_ref[...])

    @pl.when(e == N_EXPERTS - 1)
    def _():
        o_ref[...] = acc_ref[...]


def _moe_dense(layer, x, gates, w1, w3, w2):
    n, d = x.shape
    tm = _tile(n, 1024)
    row = lambda w: pl.BlockSpec((tm, w), lambda i, e: (i, 0))
    wspec = lambda r, c: pl.BlockSpec((None, None, r, c), lambda i, e: (layer, e, 0, 0))
    return pl.pallas_call(
        _moe_dense_kernel, grid=(n // tm, N_EXPERTS),
        in_specs=[row(d), row(N_EXPERTS), wspec(d, D_EXPERT), wspec(d, D_EXPERT), wspec(D_EXPERT, d)],
        out_specs=row(d), out_shape=jax.ShapeDtypeStruct((n, d), F32),
        scratch_shapes=[pltpu.VMEM((tm, d), F32)],
        compiler_params=_params("parallel", "arbitrary"), name="moe_dense")(x, gates, w1, w3, w2)


def _moe_sparse_kernel(cnt_ref, x_ref, gate_ref, tri_ref, w1_ref, w3_ref, w2_ref, o_ref, acc_ref, rank_ref):
    i, e = pl.program_id(0), pl.program_id(1)
    tm = x_ref.shape[0]

    @pl.when(e == 0)
    def _():
        acc_ref[...] = jnp.zeros(acc_ref.shape, F32)
        used = jnp.where(gate_ref[...] != 0.0, 1.0, 0.0).astype(BF16)
        rank_ref[...] = _dot(tri_ref[...], used)

    count = cnt_ref[i * N_EXPERTS + e]

    @pl.when(count > 0)
    def _():
        gate = _gate_column(gate_ref[...], e)
        rank = _gate_column(rank_ref[...], e)
        slot = lax.broadcasted_iota(jnp.int32, (1, MOE_CHUNK), 1).astype(F32)

        def chunk(c, carry):
            base = (c * MOE_CHUNK).astype(F32)
            pick = jnp.where((rank == slot + base) & (gate != 0.0), 1.0, 0.0).astype(BF16)
            xs = _dot_tn(pick, x_ref[...]).astype(BF16)
            hid = _silu(_dot(xs, w1_ref[...])) * _dot(xs, w3_ref[...])
            y = _dot(hid.astype(BF16), w2_ref[...])
            acc_ref[...] += gate * _dot(pick, y.astype(BF16))
            return carry

        lax.fori_loop(0, (count + MOE_CHUNK - 1) // MOE_CHUNK, chunk, 0)

    @pl.when(e == N_EXPERTS - 1)
    def _():
        o_ref[...] = acc_ref[...]


def _moe_sparse(layer, xb, gates, w1, w3, w2):
    n, d = xb.shape
    tm = _tile(n, 1024)
    counts = jnp.sum((gates != 0.0).reshape(n // tm, tm, N_EXPERTS), 1, dtype=jnp.int32).reshape(-1)
    tri = jnp.tril(jnp.ones((tm, tm), BF16), -1)
    row = lambda w: pl.BlockSpec((tm, w), lambda i, e, c: (i, 0))
    wspec = lambda r, c: pl.BlockSpec((None, None, r, c), lambda i, e, cnt: (layer, e, 0, 0))
    grid_spec = pltpu.PrefetchScalarGridSpec(
        num_scalar_prefetch=1, grid=(n // tm, N_EXPERTS),
        in_specs=[row(d), row(N_EXPERTS), pl.BlockSpec((tm, tm), lambda i, e, c: (0, 0)),
                  wspec(d, D_EXPERT), wspec(d, D_EXPERT), wspec(D_EXPERT, d)],
        out_specs=row(d),
        scratch_shapes=[pltpu.VMEM((tm, d), F32), pltpu.VMEM((tm, N_EXPERTS), F32)])
    return pl.pallas_call(
        _moe_sparse_kernel, grid_spec=grid_spec, out_shape=jax.ShapeDtypeStruct((n, d), F32),
        compiler_params=_params("parallel", "arbitrary"), name="moe_sparse")(counts, xb, gates, tri, w1, w3, w2)


def _res_ln_kernel(h_ref, y_ref, g_ref, b_ref, o_ref, ob_ref):
    z = _layer_norm(ALPHA * h_ref[...] + y_ref[...], g_ref[...], b_ref[...])
    o_ref[...] = z
    ob_ref[...] = z.astype(BF16)


def _res_ln(h, y, g, b):
    n, d = h.shape
    tm = _tile(n, 1024)
    row = pl.BlockSpec((tm, d), lambda i: (i, 0))
    vec = pl.BlockSpec((1, d), lambda i: (0, 0))
    return pl.pallas_call(
        _res_ln_kernel, grid=(n // tm,), in_specs=[row, row, vec, vec], out_specs=[row, row],
        out_shape=[jax.ShapeDtypeStruct((n, d), F32), jax.ShapeDtypeStruct((n, d), BF16)],
        compiler_params=_params("parallel"), name="residual_ln2")(h, y, g.reshape(1, d), b.reshape(1, d))


def _sample_proj_kernel(x_ref, rc_ref, ra_ref, rb_ref, qg_ref, kg_ref, wuq_ref, wuk_ref,
                        qlat_ref, qpe_ref, ckv_ref, kpe_ref):
    x = x_ref[...]
    rc, ra, rb = rc_ref[...], ra_ref[...], rb_ref[...]
    cq = _rms_norm(x[:, :Q_RANK], qg_ref[...])
    ckv_ref[...] = _rms_norm(x[:, Q_RANK:C_COLS], kg_ref[...])
    kpe_ref[...] = _rope_pad(x[:, C_COLS:], rc, ra, rb)
    q = _dot3(cq, wuq_ref[...])
    for h in range(H_MLA):
        lo = h * HEAD_PAD
        qlat_ref[:, h * KV_RANK:(h + 1) * KV_RANK] = _dot3(q[:, lo:lo + LANES], wuk_ref[h]).astype(BF16)
        qpe_ref[:, h * LANES:(h + 1) * LANES] = _rope_pad(q[:, lo + LANES:lo + HEAD_PAD], rc, ra, rb).astype(BF16)


def _sample_proj(proj_c, rope_c, rope_a, rope_b, qg, kg, wuq, wuk_t):
    n = proj_c.shape[0]
    return pl.pallas_call(
        _sample_proj_kernel,
        out_shape=[jax.ShapeDtypeStruct((n, H_MLA * KV_RANK), BF16), jax.ShapeDtypeStruct((n, H_MLA * LANES), BF16),
                   jax.ShapeDtypeStruct((n, KV_RANK), F32), jax.ShapeDtypeStruct((n, LANES), F32)],
        compiler_params=pltpu.CompilerParams(vmem_limit_bytes=VMEM_LIMIT), name="sample_proj",
    )(proj_c, rope_c, rope_a, rope_b, qg, kg, wuq, wuk_t)


def _ret_step_kernel(x_ref, s_ref, c_ref, sn_ref, gc_ref, gng_ref, gnb_ref, o_ref, st_ref):
    x = x_ref[...]
    c, sn = c_ref[...], sn_ref[...]
    kscale = RET_DK ** -0.5
    w = H_RET * RET_DK
    rows = 16
    row = lax.broadcasted_iota(jnp.int32, (rows, RET_DK), 0)
    for h in range(H_RET):
        q = _rope_half(x[:, h * RET_DK:(h + 1) * RET_DK], c, sn)
        k = _rope_half(x[:, w + h * RET_DK:w + (h + 1) * RET_DK], c, sn) * kscale
        v = x[:, 2 * w + h * RET_DV:2 * w + (h + 1) * RET_DV]
        g = x[:, 3 * w + h * RET_DV:3 * w + (h + 1) * RET_DV]
        k0 = jnp.where(row == 0, k, 0.0)
        s_new = gc_ref[h] * s_ref[h] + _dot3(k0, jnp.broadcast_to(v, (rows, RET_DV)), _dot_tn)
        st_ref[h] = s_new
        o = _dot3(jnp.broadcast_to(q, (rows, RET_DK)), s_new)[:1, :]
        o_ref[:, h * RET_DV:(h + 1) * RET_DV] = _ret_out(o, g, gng_ref[h], gnb_ref[h], F32)


def _retention_step(layer, proj_r, state, cos, sin, gc, gn_g, gn_b):
    bd = proj_r.shape[0]
    full = lambda a: pl.BlockSpec(a.shape, lambda i: (0,) * a.ndim)
    gng, gnb = gn_g.reshape(H_RET, 1, RET_DV), gn_b.reshape(H_RET, 1, RET_DV)
    o, st = pl.pallas_call(
        _ret_step_kernel, grid=(bd,),
        in_specs=[pl.BlockSpec((None, 1, R_COLS), lambda i: (i, 0, 0)),
                  pl.BlockSpec((None, None, H_RET, RET_DK, RET_DV), lambda i: (layer, i, 0, 0, 0)),
                  full(cos), full(sin), full(gc), full(gng), full(gnb)],
        out_specs=[pl.BlockSpec((None, 1, RET_WIDTH), lambda i: (i, 0, 0)),
                   pl.BlockSpec((None, H_RET, RET_DK, RET_DV), lambda i: (i, 0, 0, 0))],
        out_shape=[jax.ShapeDtypeStruct((bd, 1, RET_WIDTH), F32), jax.ShapeDtypeStruct(state.shape[1:], F32)],
        compiler_params=_params("parallel"), name="retention_step",
    )(proj_r.reshape(bd, 1, R_COLS), state, cos, sin, gc, gng, gnb)
    return o.reshape(bd, RET_WIDTH), st


def _decode_kernel(pt_ref, *refs, pages):
    ckv_refs, kpe_refs = refs[:pages], refs[pages:2 * pages]
    qlat_ref, qpe_ref, cnew_ref, knew_ref, o_ref, m_ref, l_ref, acc_ref, cbuf_ref, pbuf_ref = refs[2 * pages:]
    j = pl.program_id(1)

    @pl.when(j == 0)
    def _():
        m_ref[...] = jnp.full(m_ref.shape, -jnp.inf, F32)
        l_ref[...] = jnp.zeros(l_ref.shape, F32)
        acc_ref[...] = jnp.zeros(acc_ref.shape, F32)

    qlat = qlat_ref[...]
    qpe = qpe_ref[...][:, :QK_ROPE]

    for p, (c_ref, k_ref) in enumerate(zip(ckv_refs, kpe_refs)):
        cbuf_ref[p * PAGE_SIZE:(p + 1) * PAGE_SIZE, :] = c_ref[...].astype(BF16)
        pbuf_ref[:, p * PAGE_SIZE:(p + 1) * PAGE_SIZE] = k_ref[...].astype(BF16)
    ckv = cbuf_ref[...]
    s = (_dot_nt(qlat, ckv) + _dot(qpe, pbuf_ref[...])) * SM_SCALE
    m_old = m_ref[...]
    m_new = jnp.maximum(m_old, jnp.max(s, -1, keepdims=True))
    p = jnp.exp(s - m_new)
    a = jnp.exp(m_old - m_new)
    l_ref[...] = a * l_ref[...] + jnp.sum(p, -1, keepdims=True)
    acc_ref[...] = a * acc_ref[...] + _dot(p.astype(BF16), ckv)
    m_ref[...] = m_new

    @pl.when(j == pl.num_programs(1) - 1)
    def _():
        cn = jnp.broadcast_to(cnew_ref[...], (8, KV_RANK)).astype(BF16)
        kn = jnp.broadcast_to(knew_ref[...][:, :QK_ROPE], (8, QK_ROPE)).astype(BF16)
        s = (_dot_nt(qlat, cn) + _dot_nt(qpe, kn))[:, :1] * SM_SCALE
        m_old = m_ref[...]
        m_new = jnp.maximum(m_old, s)
        p = jnp.exp(s - m_new)
        a = jnp.exp(m_old - m_new)
        l = a * l_ref[...] + p
        acc = a * acc_ref[...] + p.astype(BF16).astype(F32) * cn[:1, :].astype(F32)
        o = acc / l
        for h in range(H_MLA):
            o_ref[:, h * KV_RANK:(h + 1) * KV_RANK] = o[h:h + 1, :]


def _decode_attention(layer, page_table, cache_ckv, cache_kpe_t, qlat, qpe, ckv_new, kpe_new):
    bd, n_pages = page_table.shape
    pages = _tile(n_pages, DECODE_PAGES, 1)
    ckv_spec = lambda p: pl.BlockSpec((None, None, PAGE_SIZE, KV_RANK),
                                      lambda i, j, pt: (layer, pt[i, j * pages + p], 0, 0))
    kpe_spec = lambda p: pl.BlockSpec((None, None, QK_ROPE, PAGE_SIZE),
                                      lambda i, j, pt: (layer, pt[i, j * pages + p], 0, 0))
    per_b = lambda r, w: pl.BlockSpec((None, r, w), lambda i, j, pt: (i, 0, 0))
    grid_spec = pltpu.PrefetchScalarGridSpec(
        num_scalar_prefetch=1, grid=(bd, n_pages // pages),
        in_specs=[ckv_spec(p) for p in range(pages)] + [kpe_spec(p) for p in range(pages)]
        + [per_b(H_MLA, KV_RANK), per_b(H_MLA, LANES), per_b(1, KV_RANK), per_b(1, LANES)],
        out_specs=per_b(1, H_MLA * KV_RANK),
        scratch_shapes=[pltpu.VMEM((H_MLA, 1), F32), pltpu.VMEM((H_MLA, 1), F32), pltpu.VMEM((H_MLA, KV_RANK), F32),
                        pltpu.VMEM((pages * PAGE_SIZE, KV_RANK), BF16), pltpu.VMEM((QK_ROPE, pages * PAGE_SIZE), BF16)])
    o = pl.pallas_call(
        functools.partial(_decode_kernel, pages=pages), grid_spec=grid_spec,
        out_shape=jax.ShapeDtypeStruct((bd, 1, H_MLA * KV_RANK), F32),
        compiler_params=_params("parallel", "arbitrary"), name="decode_attention",
    )(page_table, *([cache_ckv] * pages), *([cache_kpe_t] * pages),
      qlat.reshape(bd, H_MLA, KV_RANK), qpe.reshape(bd, H_MLA, LANES),
      ckv_new.reshape(bd, 1, KV_RANK), kpe_new.reshape(bd, 1, LANES))
    return o.reshape(bd, H_MLA * KV_RANK)


def _uv_kernel(o_ref, w_ref, out_ref):
    for h in range(H_MLA):
        out_ref[:, h * MLA_V:(h + 1) * MLA_V] = _dot3(o_ref[:, h * KV_RANK:(h + 1) * KV_RANK], w_ref[h])


def _uv_proj(o_lat, w_uv):
    bd = o_lat.shape[0]
    return pl.pallas_call(
        _uv_kernel, out_shape=jax.ShapeDtypeStruct((bd, MLA_WIDTH), F32),
        compiler_params=pltpu.CompilerParams(vmem_limit_bytes=VMEM_LIMIT), name="uv_proj")(o_lat, w_uv)


def _rope_tables(pos, half):
    inv = ROPE_BASE ** (-jnp.arange(half, dtype=F32) / half)
    ang = pos.astype(F32)[:, None] * inv[None, :]
    return jnp.cos(ang), jnp.sin(ang)


def _rope_half_tables(pos):
    cos, sin = _rope_tables(pos, RET_DK // 2)
    return jnp.concatenate([cos, cos], -1), jnp.concatenate([-sin, sin], -1)


def _rope_pad_tables(pos):
    cos, sin = _rope_tables(pos, QK_ROPE // 2)
    z = jnp.zeros_like(cos)
    zz = jnp.zeros((pos.shape[0], LANES - QK_ROPE), F32)
    return (jnp.concatenate([cos, cos, zz], -1), jnp.concatenate([-sin, z, zz], -1),
            jnp.concatenate([z, sin, zz], -1))


def _retention_consts():
    lg = jnp.log1p(-jnp.exp2(-5.0 - jnp.arange(H_RET, dtype=F32)))
    i = jnp.arange(RET_CHUNK, dtype=F32)
    rel = i[:, None] - i[None, :]
    dmat = jnp.where(rel[None] >= 0, jnp.exp(jnp.maximum(rel, 0.0)[None] * lg[:, None, None]), 0.0)
    cdec = jnp.exp(lg[:, None] * (i + 1.0)[None, :])
    kdec = jnp.exp(lg[:, None] * (RET_CHUNK - 1.0 - i)[None, :])
    wide = lambda a: jnp.broadcast_to(a[:, :, None], (H_RET, RET_CHUNK, LANES))
    gc = jnp.broadcast_to(jnp.exp(lg * RET_CHUNK)[:, None, None], (H_RET, 1, LANES))
    g1 = jnp.broadcast_to(jnp.exp(lg)[:, None, None], (H_RET, 1, LANES))
    return (dmat, wide(cdec), wide(kdec), gc), g1


def _layer_weights(l, w_in, w_uq, w_ukv, w_o):
    wi = w_in[l]
    w_r = wi[:, :R_COLS].astype(BF16)
    w_c32 = jnp.pad(wi[:, R_COLS:], ((0, 0), (0, LANES - QK_ROPE)))
    uq = w_uq[l].reshape(Q_RANK, H_MLA, QK_NOPE + QK_ROPE)
    uq32 = jnp.pad(uq, ((0, 0), (0, 0), (0, HEAD_PAD - QK_NOPE - QK_ROPE))).reshape(Q_RANK, H_MLA * HEAD_PAD)
    ukv = w_ukv[l].astype(BF16)
    ukv3 = w_ukv[l].reshape(KV_RANK, H_MLA, QK_NOPE + MLA_V)
    uk_t32 = jnp.transpose(ukv3[..., :QK_NOPE], (1, 2, 0))
    uv32 = jnp.transpose(ukv3[..., QK_NOPE:], (1, 0, 2))
    return dict(w_r=w_r, w_c=w_c32.astype(BF16), w_c32=w_c32, uq=uq32.astype(BF16), uq32=uq32, ukv=ukv,
                uk_t32=uk_t32, uv32=uv32, w_o=w_o[l].astype(BF16))


def kernel(x_prompt, x_sample, cache_ckv, cache_kpe, state_ret, page_table, meta_tokens, ln0_g, ln0_b, w_in, q_norm_g, w_uq, kv_norm_g, w_ukv, ret_gn_g, ret_gn_b, w_o, ln1_g, ln1_b, w_router, router_bias, w1, w3, w2, ln2_g, ln2_b):
    b, seq, d = x_prompt.shape
    bd = x_sample.shape[0]
    t = seq + N_META
    past_len = page_table.shape[1] * PAGE_SIZE
    depth = w_in.shape[0]

    ret_tab_p = _rope_half_tables(jnp.arange(t))
    ret_tab_s = _rope_half_tables(jnp.full((1,), past_len))
    pad_tab_p = _rope_pad_tables(jnp.arange(t))
    pad_tab_s = _rope_pad_tables(jnp.full((bd,), past_len))
    ret_consts, gamma1 = _retention_consts()
    cache_kpe_t = jnp.swapaxes(cache_kpe, 2, 3)
    w1_b, w3_b, w2_b = w1.astype(BF16), w3.astype(BF16), w2.astype(BF16)
    wr_hi = w_router.astype(BF16)
    wr_lo = (w_router - wr_hi.astype(F32)).astype(BF16)

    meta = jnp.broadcast_to(meta_tokens[None], (b, N_META, d))
    xp = jnp.concatenate([meta, x_prompt], 1).reshape(b * t, d)
    hp, hp_b = _ln(xp, ln0_g, ln0_b)
    hs, hs_b = _ln(x_sample.reshape(bd, d), ln0_g, ln0_b)

    outs = [[] for _ in range(6)]
    for l in range(depth):
        w = _layer_weights(l, w_in, w_uq, w_ukv, w_o)
        qg, kg = q_norm_g[l].reshape(1, Q_RANK), kv_norm_g[l].reshape(1, KV_RANK)

        proj_r = _mm(hp_b, w["w_r"], 1024, "proj_r")
        proj_c = _mm(hp_b, w["w_c"], C_COLS + LANES, "proj_c")
        q, k, v, ckv, kpe = _mla_proj(proj_c, *pad_tab_p, qg, kg, w["uq"], w["ukv"])
        ret_o, ret_s = _retention_prompt(proj_r, *ret_tab_p, ret_consts, ret_gn_g[l], ret_gn_b[l], b, t)
        mla_o = _attention_prompt(q, k, v, b, t)
        h1, h1_b, gates = _oproj(ret_o.reshape(b * t, -1), mla_o.reshape(b * t, -1), hp, w["w_o"],
                                 ln1_g[l], ln1_b[l], wr_hi, wr_lo, router_bias)
        hp, hp_b = _res_ln(h1, _moe_sparse(l, h1_b, gates, w1_b, w3_b, w2_b), ln2_g[l], ln2_b[l])
        outs[0].append(ckv.reshape(b, t, KV_RANK))
        outs[1].append(kpe[:, :QK_ROPE].reshape(b, t, QK_ROPE))
        outs[2].append(ret_s)

        proj_r = _mm(hs, w_in, 1024, "proj_r_s", precise=True, layer=l, ncols=R_COLS)
        proj_c = _mm(hs, w["w_c32"], C_COLS + LANES, "proj_c_s", precise=True)
        qlat, qpe, ckv, kpe = _sample_proj(proj_c, *pad_tab_s, qg, kg, w["uq32"], w["uk_t32"])
        ret_o, ret_s = _retention_step(l, proj_r, state_ret, *ret_tab_s, gamma1, ret_gn_g[l], ret_gn_b[l])
        o_lat = _decode_attention(l, page_table, cache_ckv, cache_kpe_t, qlat, qpe, ckv, kpe)
        mla_o = _uv_proj(o_lat, w["uv32"])
        h1, _, gates = _oproj(ret_o, mla_o, hs, w_o, ln1_g[l], ln1_b[l], wr_hi, wr_lo, router_bias, layer=l)
        hs, _ = _res_ln(h1, _moe_dense(l, h1, gates, w1, w3, w2), ln2_g[l], ln2_b[l])
        outs[3].append(ckv.reshape(bd, 1, KV_RANK))
        outs[4].append(kpe[:, :QK_ROPE].reshape(bd, 1, QK_ROPE))
        outs[5].append(ret_s)

    y_prompt = hp.reshape(b, t, d)[:, N_META:]
    y_sample = hs.reshape(bd, 1, d)
    return (y_prompt, y_sample) + tuple(jnp.stack(o, 0) for o in outs)
```

```python
import functools
import math

import jax
import jax.numpy as jnp
from jax import lax
from jax.experimental import pallas as pl
from jax.experimental.pallas import tpu as pltpu

D_MODEL = 2048
N_META = 16
PAGE_SIZE = 128
H_RET = 8
RET_DK = 128
RET_DV = 128
RET_WIDTH = H_RET * RET_DV
RET_CHUNK = 128
H_MLA = 8
MLA_V = 128
MLA_WIDTH = H_MLA * MLA_V
QK_NOPE = 128
QK_ROPE = 64
Q_RANK = 512
KV_RANK = 512
SM_SCALE = (QK_NOPE + QK_ROPE) ** -0.5
N_EXPERTS = 16
N_GROUPS = 4
EXPERTS_PER_GROUP = N_EXPERTS // N_GROUPS
D_EXPERT = 512
ROPE_BASE = 10000.0
NORM_EPS = 1e-5
DEPTH = 2
ALPHA = (2 * DEPTH) ** 0.25
R_COLS = 2 * H_RET * RET_DK + 2 * RET_WIDTH
C_COLS = Q_RANK + KV_RANK
LANES = 128
HEAD_PAD = 2 * LANES
VMEM_LIMIT = 56 * 1024 * 1024
DECODE_PAGES = 16
RET_HEADS_PER_STEP = 2
MOE_CHUNK = 128
RET_STEP_SEQS = 4

BF16 = jnp.bfloat16
F32 = jnp.float32


def _tile(n, cap, align=16):
    best = None
    for t in range(align, min(n, cap) + 1, align):
        if n % t == 0:
            best = t
    return best if best is not None else n


def _params(*sem):
    return pltpu.CompilerParams(dimension_semantics=sem, vmem_limit_bytes=VMEM_LIMIT)


def _dot(a, b):
    return jnp.dot(a, b, preferred_element_type=F32)


def _dot_nt(a, b):
    return lax.dot_general(a, b, (((1,), (1,)), ((), ())), preferred_element_type=F32)


def _dot_tn(a, b):
    return lax.dot_general(a, b, (((0,), (0,)), ((), ())), preferred_element_type=F32)


def _split(x):
    hi = x.astype(BF16)
    return hi, (x - hi.astype(F32)).astype(BF16)


def _dot3(a, b, dot=_dot):
    ah, al = _split(a)
    bh, bl = _split(b)
    return dot(ah, bh) + (dot(al, bh) + dot(ah, bl))


def _layer_norm(x, g, b):
    mu = jnp.mean(x, -1, keepdims=True)
    xc = x - mu
    var = jnp.mean(xc * xc, -1, keepdims=True)
    return xc * lax.rsqrt(var + NORM_EPS) * g + b


def _rms_norm(x, g):
    return x * lax.rsqrt(jnp.mean(x * x, -1, keepdims=True) + NORM_EPS) * g


def _silu(x):
    return x * (1.0 / (1.0 + jnp.exp(-x)))


def _rope_half(x, c, s):
    return x * c + pltpu.roll(x, LANES // 2, 1) * s


def _rope_pad(u, c, a, b):
    return u * c + pltpu.roll(u, LANES - QK_ROPE // 2, 1) * a + pltpu.roll(u, QK_ROPE // 2, 1) * b


def _ln_kernel(x_ref, g_ref, b_ref, o_ref, ob_ref):
    y = _layer_norm(x_ref[...], g_ref[...], b_ref[...])
    o_ref[...] = y
    ob_ref[...] = y.astype(BF16)


def _ln(x, g, b):
    n, d = x.shape
    tm = _tile(n, 1024)
    row = pl.BlockSpec((tm, d), lambda i: (i, 0))
    vec = pl.BlockSpec((1, d), lambda i: (0, 0))
    return pl.pallas_call(
        _ln_kernel, grid=(n // tm,), in_specs=[row, vec, vec], out_specs=[row, row],
        out_shape=[jax.ShapeDtypeStruct((n, d), F32), jax.ShapeDtypeStruct((n, d), BF16)],
        compiler_params=_params("parallel"), name="ln0")(x, g.reshape(1, d), b.reshape(1, d))


def _mm_kernel(x_ref, w_ref, o_ref, *, precise):
    o_ref[...] = (_dot3 if precise else _dot)(x_ref[...], w_ref[...])


def _mm(x, w, tn, name, precise=False, layer=None, ncols=None):
    n, k = x.shape
    nw = ncols if ncols is not None else w.shape[-1]
    tm = _tile(n, 1024)
    if layer is None:
        w_spec = pl.BlockSpec((k, tn), lambda j, i: (0, j))
    else:
        w_spec = pl.BlockSpec((None, k, tn), lambda j, i: (layer, 0, j))
    return pl.pallas_call(
        functools.partial(_mm_kernel, precise=precise), grid=(nw // tn, n // tm),
        in_specs=[pl.BlockSpec((tm, k), lambda j, i: (i, 0)), w_spec],
        out_specs=pl.BlockSpec((tm, tn), lambda j, i: (i, j)),
        out_shape=jax.ShapeDtypeStruct((n, nw), F32),
        compiler_params=_params("parallel", "parallel"), name=name)(x, w)


def _mla_proj_kernel(x_ref, rc_ref, ra_ref, rb_ref, qg_ref, kg_ref, wuq_ref, wukv_ref,
                     q_ref, k_ref, v_ref, ckv_ref, kpe_ref):
    x = x_ref[...]
    rc, ra, rb = rc_ref[...], ra_ref[...], rb_ref[...]
    cq = _rms_norm(x[:, :Q_RANK], qg_ref[...]).astype(BF16)
    ckv = _rms_norm(x[:, Q_RANK:C_COLS], kg_ref[...])
    kpe = _rope_pad(x[:, C_COLS:], rc, ra, rb)
    ckv_ref[...] = ckv
    kpe_ref[...] = kpe
    kpe_b = kpe.astype(BF16)
    q = _dot(cq, wuq_ref[...])
    kv = _dot(ckv.astype(BF16), wukv_ref[...])
    for h in range(H_MLA):
        lo = h * HEAD_PAD
        q_ref[:, lo:lo + LANES] = q[:, lo:lo + LANES].astype(BF16)
        q_ref[:, lo + LANES:lo + HEAD_PAD] = _rope_pad(q[:, lo + LANES:lo + HEAD_PAD], rc, ra, rb).astype(BF16)
        k_ref[:, lo:lo + LANES] = kv[:, lo:lo + LANES].astype(BF16)
        k_ref[:, lo + LANES:lo + HEAD_PAD] = kpe_b
        v_ref[:, h * MLA_V:(h + 1) * MLA_V] = kv[:, lo + LANES:lo + HEAD_PAD].astype(BF16)


def _mla_proj(proj_c, rope_c, rope_a, rope_b, qg, kg, wuq, wukv):
    n, t = proj_c.shape[0], rope_c.shape[0]
    tm = _tile(t, 1024)
    row = lambda w: pl.BlockSpec((tm, w), lambda i: (i, 0))
    tab = pl.BlockSpec((tm, LANES), lambda i: (i % (t // tm), 0))
    full = lambda a: pl.BlockSpec(a.shape, lambda i: (0,) * a.ndim)
    return pl.pallas_call(
        _mla_proj_kernel, grid=(n // tm,),
        in_specs=[row(C_COLS + LANES), tab, tab, tab, full(qg), full(kg), full(wuq), full(wukv)],
        out_specs=[row(H_MLA * HEAD_PAD), row(H_MLA * HEAD_PAD), row(MLA_WIDTH), row(KV_RANK), row(LANES)],
        out_shape=[jax.ShapeDtypeStruct((n, H_MLA * HEAD_PAD), BF16), jax.ShapeDtypeStruct((n, H_MLA * HEAD_PAD), BF16),
                   jax.ShapeDtypeStruct((n, MLA_WIDTH), BF16), jax.ShapeDtypeStruct((n, KV_RANK), F32),
                   jax.ShapeDtypeStruct((n, LANES), F32)],
        compiler_params=_params("parallel"), name="mla_proj")(proj_c, rope_c, rope_a, rope_b, qg, kg, wuq, wukv)


def _ret_chunk(s, q, k, v, dmat, cdec, kdec, gc):
    qb, vb = q.astype(BF16), v.astype(BF16)
    inner = _dot((_dot_nt(qb, k.astype(BF16)) * dmat).astype(BF16), vb)
    cross = _dot(qb, s.astype(BF16)) * cdec
    s_new = gc * s + _dot_tn((k * kdec).astype(BF16), vb)
    return inner + cross, s_new


def _ret_out(o, g, gn_g, gn_b, dtype=BF16):
    mu = jnp.mean(o, -1, keepdims=True)
    oc = o - mu
    var = jnp.mean(oc * oc, -1, keepdims=True)
    on = oc * lax.rsqrt(var + NORM_EPS) * gn_g + gn_b
    return (_silu(g) * on).astype(dtype)


def _ret_kernel(q_ref, k_ref, v_ref, g_ref, c_ref, s_ref, dmat_ref, cdec_ref, kdec_ref, gc_ref, gng_ref, gnb_ref,
                o_ref, st_ref, pad_ref, *, n_chunks):
    kscale = RET_DK ** -0.5
    heads = range(RET_HEADS_PER_STEP)
    consts = [(dmat_ref[h], cdec_ref[h], kdec_ref[h], gc_ref[h]) for h in heads]

    def load(h, r0, rows):
        sl, ln = pl.ds(r0, rows), slice(h * LANES, (h + 1) * LANES)
        c, s = c_ref[sl, :], s_ref[sl, :]
        return (_rope_half(q_ref[sl, ln], c, s), _rope_half(k_ref[sl, ln], c, s) * kscale, v_ref[sl, ln], g_ref[sl, ln])

    lead = RET_CHUNK - N_META
    pad_ref[...] = jnp.zeros(pad_ref.shape, F32)
    states = []
    for h in heads:
        q0, k0, v0, g0 = load(h, 0, N_META)
        pad_ref[h, 0, lead:, :] = q0
        pad_ref[h, 1, lead:, :] = k0
        pad_ref[h, 2, lead:, :] = v0
        o0, s1 = _ret_chunk(jnp.zeros((RET_DK, RET_DV), F32), pad_ref[h, 0], pad_ref[h, 1], pad_ref[h, 2], *consts[h])
        o_ref[0:N_META, h * LANES:(h + 1) * LANES] = _ret_out(o0[lead:, :], g0, gng_ref[h], gnb_ref[h])
        states.append(s1)

    def body(c, states):
        r0 = pl.multiple_of(N_META + c * RET_CHUNK, 16)
        new = []
        for h in heads:
            q, k, v, g = load(h, r0, RET_CHUNK)
            o, s = _ret_chunk(states[h], q, k, v, *consts[h])
            o_ref[pl.ds(r0, RET_CHUNK), h * LANES:(h + 1) * LANES] = _ret_out(o, g, gng_ref[h], gnb_ref[h])
            new.append(s)
        return tuple(new)

    states = lax.fori_loop(0, n_chunks, body, tuple(states))
    for h in heads:
        st_ref[h] = states[h]


def _retention_prompt(proj_r, cos, sin, consts, gn_g, gn_b, b, t):
    dmat, cdec, kdec, gc = consts
    x = proj_r.reshape(b, t, R_COLS)
    hs, groups = RET_HEADS_PER_STEP, H_RET // RET_HEADS_PER_STEP
    head = lambda part: pl.BlockSpec((None, t, hs * LANES), lambda i, h: (i, 0, part * groups + h))
    tab = pl.BlockSpec((t, LANES), lambda i, h: (0, 0))
    per_h = lambda r: pl.BlockSpec((hs, r, LANES), lambda i, h: (h, 0, 0))
    return pl.pallas_call(
        functools.partial(_ret_kernel, n_chunks=(t - N_META) // RET_CHUNK), grid=(b, groups),
        in_specs=[head(0), head(1), head(2), head(3), tab, tab,
                  per_h(RET_CHUNK), per_h(RET_CHUNK), per_h(RET_CHUNK), per_h(1), per_h(1), per_h(1)],
        out_specs=[pl.BlockSpec((None, t, hs * LANES), lambda i, h: (i, 0, h)),
                   pl.BlockSpec((None, hs, RET_DK, RET_DV), lambda i, h: (i, h, 0, 0))],
        out_shape=[jax.ShapeDtypeStruct((b, t, RET_WIDTH), BF16), jax.ShapeDtypeStruct((b, H_RET, RET_DK, RET_DV), F32)],
        scratch_shapes=[pltpu.VMEM((hs, 3, RET_CHUNK, LANES), F32)],
        compiler_params=_params("parallel", "parallel"), name="retention_prompt",
    )(x, x, x, x, cos, sin, dmat, cdec, kdec, gc, gn_g.reshape(H_RET, 1, RET_DV), gn_b.reshape(H_RET, 1, RET_DV))


def _attn_kernel(q_ref, k_ref, v_ref, o_ref, *, t, blk):
    n_full = t // blk
    starts = [(i * blk, blk) for i in range(n_full)]
    if t % blk:
        starts.append((n_full * blk, t % blk))
    for qi, (q0, qn) in enumerate(starts):
        q = q_ref[q0:q0 + qn, :]
        m = jnp.full((qn, 1), -jnp.inf, F32)
        l = jnp.zeros((qn, 1), F32)
        acc = jnp.zeros((qn, MLA_V), F32)
        for ki, (k0, kn) in enumerate(starts[:qi + 1]):
            s = _dot_nt(q, k_ref[k0:k0 + kn, :]) * SM_SCALE
            if ki == qi:
                rows = lax.broadcasted_iota(jnp.int32, (qn, kn), 0)
                cols = lax.broadcasted_iota(jnp.int32, (qn, kn), 1)
                s = jnp.where(cols <= rows, s, -jnp.inf)
            m_new = jnp.maximum(m, jnp.max(s, -1, keepdims=True))
            p = jnp.exp(s - m_new)
            a = jnp.exp(m - m_new)
            l = a * l + jnp.sum(p, -1, keepdims=True)
            acc = a * acc + _dot(p.astype(BF16), v_ref[k0:k0 + kn, :])
            m = m_new
        o_ref[q0:q0 + qn, :] = (acc / l).astype(BF16)


def _attention_prompt(q, k, v, b, t):
    blk = 512 if t >= 512 else 128
    spec = lambda w: pl.BlockSpec((None, t, w), lambda i, h: (i, 0, h))
    return pl.pallas_call(
        functools.partial(_attn_kernel, t=t, blk=blk), grid=(b, H_MLA),
        in_specs=[spec(HEAD_PAD), spec(HEAD_PAD), spec(MLA_V)], out_specs=spec(MLA_V),
        out_shape=jax.ShapeDtypeStruct((b, t, MLA_WIDTH), BF16),
        compiler_params=_params("parallel", "parallel"), name="attention_prompt",
    )(q.reshape(b, t, -1), k.reshape(b, t, -1), v.reshape(b, t, -1))


def _route(h, wr_hi, wr_lo, bias):
    hi = h.astype(BF16)
    lo = (h - hi.astype(F32)).astype(BF16)
    logits = _dot(hi, wr_hi) + (_dot(lo, wr_hi) + _dot(hi, wr_lo))
    scores = 1.0 / (1.0 + jnp.exp(-logits))
    sel = scores + bias
    lane = lax.broadcasted_iota(jnp.int32, sel.shape, 1).astype(F32)
    neg = -jnp.inf

    def top2(vals):
        t1 = jnp.max(vals, -1, keepdims=True)
        i1 = jnp.min(jnp.where(vals == t1, lane, float(N_EXPERTS)), -1, keepdims=True)
        rest = jnp.where(lane == i1, neg, vals)
        t2 = jnp.max(rest, -1, keepdims=True)
        i2 = jnp.min(jnp.where(rest == t2, lane, float(N_EXPERTS)), -1, keepdims=True)
        return t1, i1, t2, i2

    def in_group(g):
        lo = g * float(EXPERTS_PER_GROUP)
        return jnp.where(lane >= lo, lane, float(N_EXPERTS)) < lo + EXPERTS_PER_GROUP

    best_score = best = None
    for g in range(N_GROUPS):
        t1, _, t2, _ = top2(jnp.where(in_group(float(g)), sel, neg))
        gs = t1 + t2
        if g == 0:
            best_score, best = gs, jnp.zeros_like(gs)
        else:
            upd = gs > best_score
            best_score = jnp.where(upd, gs, best_score)
            best = jnp.where(upd, float(g), best)
    in_best = in_group(best)
    _, i1, _, i2 = top2(jnp.where(in_best, sel, neg))
    s1 = jnp.sum(jnp.where(lane == i1, scores, 0.0), -1, keepdims=True)
    s2 = jnp.sum(jnp.where(lane == i2, scores, 0.0), -1, keepdims=True)
    tot = s1 + s2
    return jnp.where(lane == i1, s1 / tot, 0.0) + jnp.where(lane == i2, s2 / tot, 0.0)


def _oproj_kernel(r_ref, a_ref, h_ref, wa_ref, wb_ref, g_ref, b_ref, whi_ref, wlo_ref, rb_ref,
                  o_ref, ob_ref, gate_ref, *, precise):
    dot = _dot3 if precise else _dot
    y = dot(r_ref[...], wa_ref[...]) + dot(a_ref[...], wb_ref[...])
    h1 = _layer_norm(ALPHA * h_ref[...] + y, g_ref[...], b_ref[...])
    o_ref[...] = h1
    ob_ref[...] = h1.astype(BF16)
    gate_ref[...] = _route(h1, whi_ref[...], wlo_ref[...], rb_ref[...])


def _oproj(ret_o, mla_o, h, w_o, g, b, wr_hi, wr_lo, rbias, layer=None):
    n, d = h.shape
    tm = _tile(n, 512)
    half = w_o.shape[-2] // 2
    row = lambda w: pl.BlockSpec((tm, w), lambda i: (i, 0))
    vec = lambda w: pl.BlockSpec((1, w), lambda i: (0, 0))
    if layer is None:
        wspec = lambda j: pl.BlockSpec((half, d), lambda i: (j, 0))
    else:
        wspec = lambda j: pl.BlockSpec((None, half, d), lambda i: (layer, j, 0))
    rspec = pl.BlockSpec((d, N_EXPERTS), lambda i: (0, 0))
    return pl.pallas_call(
        functools.partial(_oproj_kernel, precise=layer is not None), grid=(n // tm,),
        in_specs=[row(half), row(half), row(d), wspec(0), wspec(1), vec(d), vec(d), rspec, rspec, vec(N_EXPERTS)],
        out_specs=[row(d), row(d), row(N_EXPERTS)],
        out_shape=[jax.ShapeDtypeStruct((n, d), F32), jax.ShapeDtypeStruct((n, d), BF16),
                   jax.ShapeDtypeStruct((n, N_EXPERTS), F32)],
        compiler_params=_params("parallel"), name="oproj_ln1_router",
    )(ret_o, mla_o, h, w_o, w_o, g.reshape(1, d), b.reshape(1, d), wr_hi, wr_lo, rbias.reshape(1, N_EXPERTS))


def _gate_column(gates, e):
    lane = lax.broadcasted_iota(jnp.int32, gates.shape, 1)
    return jnp.sum(jnp.where(lane == e, gates, 0.0), -1, keepdims=True)


def _moe_dense_kernel(x_ref, gate_ref, w1_ref, w3_ref, w2_ref, o_ref, acc_ref):
    e = pl.program_id(1)

    @pl.when(e == 0)
    def _():
        acc_ref[...] = jnp.zeros(acc_ref.shape, F32)

    x = x_ref[...]
    hid = _silu(_dot3(x, w1_ref[...])) * _dot3(x, w3_ref[...])
    acc_ref[...] += _dot3(hid * _gate_column(gate_ref[...], e), w2_ref[...])

    @pl.when(e == N_EXPERTS - 1)
    def _():
        o_ref[...] = acc_ref[...]


def _moe_dense(layer, x, gates, w1, w3, w2):
    n, d = x.shape
    tm = _tile(n, 1024)
    row = lambda w: pl.BlockSpec((tm, w), lambda i, e: (i, 0))
    wspec = lambda r, c: pl.BlockSpec((None, None, r, c), lambda i, e: (layer, e, 0, 0))
    return pl.pallas_call(
        _moe_dense_kernel, grid=(n // tm, N_EXPERTS),
        in_specs=[row(d), row(N_EXPERTS), wspec(d, D_EXPERT), wspec(d, D_EXPERT), wspec(D_EXPERT, d)],
        out_specs=row(d), out_shape=jax.ShapeDtypeStruct((n, d), F32),
        scratch_shapes=[pltpu.VMEM((tm, d), F32)],
        compiler_params=_params("parallel", "arbitrary"), name="moe_dense")(x, gates, w1, w3, w2)


def _moe_sparse_kernel(cnt_ref, x_ref, gate_ref, tri_ref, w1_ref, w3_ref, w2_ref, o_ref, acc_ref, rank_ref):
    i, e = pl.program_id(0), pl.program_id(1)
    tm = x_ref.shape[0]

    @pl.when(e == 0)
    def _():
        acc_ref[...] = jnp.zeros(acc_ref.shape, F32)
        used = jnp.where(gate_ref[...] != 0.0, 1.0, 0.0).astype(BF16)
        rank_ref[...] = _dot(tri_ref[...], used)

    count = cnt_ref[i * N_EXPERTS + e]

    @pl.when(count > 0)
    def _():
        gate = _gate_column(gate_ref[...], e)
        rank = _gate_column(rank_ref[...], e)
        slot = lax.broadcasted_iota(jnp.int32, (1, MOE_CHUNK), 1).astype(F32)

        def chunk(c, carry):
            base = (c * MOE_CHUNK).astype(F32)
            pick = jnp.where((rank == slot + base) & (gate != 0.0), 1.0, 0.0).astype(BF16)
            xs = _dot_tn(pick, x_ref[...]).astype(BF16)
            hid = _silu(_dot(xs, w1_ref[...])) * _dot(xs, w3_ref[...])
            y = _dot(hid.astype(BF16), w2_ref[...])
            acc_ref[...] += gate * _dot(pick, y.astype(BF16))
            return carry

        lax.fori_loop(0, (count + MOE_CHUNK - 1) // MOE_CHUNK, chunk, 0)

    @pl.when(e == N_EXPERTS - 1)
    def _():
        o_ref[...] = acc_ref[...]


def _moe_sparse(layer, xb, gates, w1, w3, w2):
    n, d = xb.shape
    tm = _tile(n, 1024)
    counts = jnp.sum((gates != 0.0).reshape(n // tm, tm, N_EXPERTS), 1, dtype=jnp.int32).reshape(-1)
    tri = jnp.tril(jnp.ones((tm, tm), BF16), -1)
    row = lambda w: pl.BlockSpec((tm, w), lambda i, e, c: (i, 0))
    wspec = lambda r, c: pl.BlockSpec((None, None, r, c), lambda i, e, cnt: (layer, e, 0, 0))
    grid_spec = pltpu.PrefetchScalarGridSpec(
        num_scalar_prefetch=1, grid=(n // tm, N_EXPERTS),
        in_specs=[row(d), row(N_EXPERTS), pl.BlockSpec((tm, tm), lambda i, e, c: (0, 0)),
                  wspec(d, D_EXPERT), wspec(d, D_EXPERT), wspec(D_EXPERT, d)],
        out_specs=row(d),
        scratch_shapes=[pltpu.VMEM((tm, d), F32), pltpu.VMEM((tm, N_EXPERTS), F32)])
    return pl.pallas_call(
        _moe_sparse_kernel, grid_spec=grid_spec, out_shape=jax.ShapeDtypeStruct((n, d), F32),
        compiler_params=_params("parallel", "arbitrary"), name="moe_sparse")(counts, xb, gates, tri, w1, w3, w2)


def _res_ln_kernel(h_ref, y_ref, g_ref, b_ref, o_ref, ob_ref):
    z = _layer_norm(ALPHA * h_ref[...] + y_ref[...], g_ref[...], b_ref[...])
    o_ref[...] = z
    ob_ref[...] = z.astype(BF16)


def _res_ln(h, y, g, b):
    n, d = h.shape
    tm = _tile(n, 1024)
    row = pl.BlockSpec((tm, d), lambda i: (i, 0))
    vec = pl.BlockSpec((1, d), lambda i: (0, 0))
    return pl.pallas_call(
        _res_ln_kernel, grid=(n // tm,), in_specs=[row, row, vec, vec], out_specs=[row, row],
        out_shape=[jax.ShapeDtypeStruct((n, d), F32), jax.ShapeDtypeStruct((n, d), BF16)],
        compiler_params=_params("parallel"), name="residual_ln2")(h, y, g.reshape(1, d), b.reshape(1, d))


def _sample_proj_kernel(x_ref, rc_ref, ra_ref, rb_ref, qg_ref, kg_ref, wuq_ref, wuk_ref,
                        qlat_ref, qpe_ref, ckv_ref, kpe_ref):
    x = x_ref[...]
    rc, ra, rb = rc_ref[...], ra_ref[...], rb_ref[...]
    cq = _rms_norm(x[:, :Q_RANK], qg_ref[...])
    ckv_ref[...] = _rms_norm(x[:, Q_RANK:C_COLS], kg_ref[...])
    kpe_ref[...] = _rope_pad(x[:, C_COLS:], rc, ra, rb)
    q = _dot3(cq, wuq_ref[...])
    for h in range(H_MLA):
        lo = h * HEAD_PAD
        qlat_ref[:, h * KV_RANK:(h + 1) * KV_RANK] = _dot3(q[:, lo:lo + LANES], wuk_ref[h]).astype(BF16)
        qpe_ref[:, h * LANES:(h + 1) * LANES] = _rope_pad(q[:, lo + LANES:lo + HEAD_PAD], rc, ra, rb).astype(BF16)


def _sample_proj(proj_c, rope_c, rope_a, rope_b, qg, kg, wuq, wuk_t):
    n = proj_c.shape[0]
    return pl.pallas_call(
        _sample_proj_kernel,
        out_shape=[jax.ShapeDtypeStruct((n, H_MLA * KV_RANK), BF16), jax.ShapeDtypeStruct((n, H_MLA * LANES), BF16),
                   jax.ShapeDtypeStruct((n, KV_RANK), F32), jax.ShapeDtypeStruct((n, LANES), F32)],
        compiler_params=pltpu.CompilerParams(vmem_limit_bytes=VMEM_LIMIT), name="sample_proj",
    )(proj_c, rope_c, rope_a, rope_b, qg, kg, wuq, wuk_t)


def _ret_step_kernel(x_ref, s_ref, c_ref, sn_ref, gc_ref, gng_ref, gnb_ref, o_ref, st_ref):
    c, sn = c_ref[...], sn_ref[...]
    kscale = RET_DK ** -0.5
    w = H_RET * RET_DK
    rows = 16
    row = lax.broadcasted_iota(jnp.int32, (rows, RET_DK), 0)
    for b in range(x_ref.shape[0]):
        x = x_ref[b]
        for h in range(H_RET):
            q = _rope_half(x[:, h * RET_DK:(h + 1) * RET_DK], c, sn)
            k = _rope_half(x[:, w + h * RET_DK:w + (h + 1) * RET_DK], c, sn) * kscale
            v = x[:, 2 * w + h * RET_DV:2 * w + (h + 1) * RET_DV]
            g = x[:, 3 * w + h * RET_DV:3 * w + (h + 1) * RET_DV]
            k0 = jnp.where(row == 0, k, 0.0)
            s_new = gc_ref[h] * s_ref[b, h] + _dot3(k0, jnp.broadcast_to(v, (rows, RET_DV)), _dot_tn)
            st_ref[b, h] = s_new
            o = _dot3(jnp.broadcast_to(q, (rows, RET_DK)), s_new)[:1, :]
            o_ref[b, :, h * RET_DV:(h + 1) * RET_DV] = _ret_out(o, g, gng_ref[h], gnb_ref[h], F32)


def _retention_step(layer, proj_r, state, cos, sin, gc, gn_g, gn_b):
    bd = proj_r.shape[0]
    nb = _tile(bd, RET_STEP_SEQS, 1)
    full = lambda a: pl.BlockSpec(a.shape, lambda i: (0,) * a.ndim)
    gng, gnb = gn_g.reshape(H_RET, 1, RET_DV), gn_b.reshape(H_RET, 1, RET_DV)
    o, st = pl.pallas_call(
        _ret_step_kernel, grid=(bd // nb,),
        in_specs=[pl.BlockSpec((nb, 1, R_COLS), lambda i: (i, 0, 0)),
                  pl.BlockSpec((None, nb, H_RET, RET_DK, RET_DV), lambda i: (layer, i, 0, 0, 0)),
                  full(cos), full(sin), full(gc), full(gng), full(gnb)],
        out_specs=[pl.BlockSpec((nb, 1, RET_WIDTH), lambda i: (i, 0, 0)),
                   pl.BlockSpec((nb, H_RET, RET_DK, RET_DV), lambda i: (i, 0, 0, 0))],
        out_shape=[jax.ShapeDtypeStruct((bd, 1, RET_WIDTH), F32), jax.ShapeDtypeStruct(state.shape[1:], F32)],
        compiler_params=_params("parallel"), name="retention_step",
    )(proj_r.reshape(bd, 1, R_COLS), state, cos, sin, gc, gng, gnb)
    return o.reshape(bd, RET_WIDTH), st


def _decode_kernel(pt_ref, qlat_ref, qpe_ref, cnew_ref, knew_ref, ckv_hbm, kpe_hbm, o_ref,
                   m_ref, l_ref, acc_ref, cbuf_ref, kbuf_ref, sem, *, layer, pages):
    i, j = pl.program_id(0), pl.program_id(1)
    n_j = pl.num_programs(1)
    step = i * n_j + j
    slot = step % 2

    def page_copies(seq, group, s):
        copies = []
        for p in range(pages):
            page = pt_ref[seq, group * pages + p]
            copies.append(pltpu.make_async_copy(ckv_hbm.at[layer, page], cbuf_ref.at[s, p], sem.at[s]))
            copies.append(pltpu.make_async_copy(kpe_hbm.at[layer, page], kbuf_ref.at[s, p], sem.at[s]))
        return copies

    @pl.when(step == 0)
    def _():
        for c in page_copies(i, j, slot):
            c.start()

    wrap = j == n_j - 1

    @pl.when(step < pl.num_programs(0) * n_j - 1)
    def _():
        for c in page_copies(jnp.where(wrap, i + 1, i), jnp.where(wrap, 0, j + 1), 1 - slot):
            c.start()

    for c in page_copies(i, j, slot):
        c.wait()

    @pl.when(j == 0)
    def _():
        m_ref[...] = jnp.full(m_ref.shape, -jnp.inf, F32)
        l_ref[...] = jnp.zeros(l_ref.shape, F32)
        acc_ref[...] = jnp.zeros(acc_ref.shape, F32)

    qlat = qlat_ref[...]
    qpe = qpe_ref[...][:, :QK_ROPE]
    qlat32, qpe32 = qlat.astype(F32), qpe.astype(F32)

    s = jnp.concatenate([_dot_nt(qlat32, cbuf_ref[slot, p]) + _dot(qpe32, kbuf_ref[slot, p])
                         for p in range(pages)], -1) * SM_SCALE
    m_old = m_ref[...]
    m_new = jnp.maximum(m_old, jnp.max(s, -1, keepdims=True))
    p_all = jnp.exp(s - m_new)
    a = jnp.exp(m_old - m_new)
    l_ref[...] = a * l_ref[...] + jnp.sum(p_all, -1, keepdims=True)
    pv = _dot(p_all[:, :PAGE_SIZE], cbuf_ref[slot, 0])
    for p in range(1, pages):
        pv += _dot(p_all[:, p * PAGE_SIZE:(p + 1) * PAGE_SIZE], cbuf_ref[slot, p])
    acc_ref[...] = a * acc_ref[...] + pv
    m_ref[...] = m_new

    @pl.when(j == pl.num_programs(1) - 1)
    def _():
        cn = jnp.broadcast_to(cnew_ref[...], (8, KV_RANK)).astype(BF16)
        kn = jnp.broadcast_to(knew_ref[...][:, :QK_ROPE], (8, QK_ROPE)).astype(BF16)
        s = (_dot_nt(qlat, cn) + _dot_nt(qpe, kn))[:, :1] * SM_SCALE
        m_old = m_ref[...]
        m_new = jnp.maximum(m_old, s)
        p = jnp.exp(s - m_new)
        a = jnp.exp(m_old - m_new)
        l = a * l_ref[...] + p
        acc = a * acc_ref[...] + p.astype(BF16).astype(F32) * cn[:1, :].astype(F32)
        o = acc / l
        for h in range(H_MLA):
            o_ref[:, h * KV_RANK:(h + 1) * KV_RANK] = o[h:h + 1, :]


def _decode_attention(layer, page_table, cache_ckv, cache_kpe_t, qlat, qpe, ckv_new, kpe_new):
    bd, n_pages = page_table.shape
    pages = _tile(n_pages, DECODE_PAGES, 1)
    per_b = lambda r, w: pl.BlockSpec((None, r, w), lambda i, j, pt: (i, 0, 0))
    in_hbm = pl.BlockSpec(memory_space=pl.ANY)
    grid_spec = pltpu.PrefetchScalarGridSpec(
        num_scalar_prefetch=1, grid=(bd, n_pages // pages),
        in_specs=[per_b(H_MLA, KV_RANK), per_b(H_MLA, LANES), per_b(1, KV_RANK), per_b(1, LANES), in_hbm, in_hbm],
        out_specs=per_b(1, H_MLA * KV_RANK),
        scratch_shapes=[pltpu.VMEM((H_MLA, 1), F32), pltpu.VMEM((H_MLA, 1), F32), pltpu.VMEM((H_MLA, KV_RANK), F32),
                        pltpu.VMEM((2, pages, PAGE_SIZE, KV_RANK), F32), pltpu.VMEM((2, pages, QK_ROPE, PAGE_SIZE), F32),
                        pltpu.SemaphoreType.DMA((2,))])
    o = pl.pallas_call(
        functools.partial(_decode_kernel, layer=layer, pages=pages), grid_spec=grid_spec,
        out_shape=jax.ShapeDtypeStruct((bd, 1, H_MLA * KV_RANK), F32),
        compiler_params=_params("arbitrary", "arbitrary"), name="decode_attention",
    )(page_table, qlat.reshape(bd, H_MLA, KV_RANK), qpe.reshape(bd, H_MLA, LANES),
      ckv_new.reshape(bd, 1, KV_RANK), kpe_new.reshape(bd, 1, LANES), cache_ckv, cache_kpe_t)
    return o.reshape(bd, H_MLA * KV_RANK)


def _uv_kernel(o_ref, w_ref, out_ref):
    for h in range(H_MLA):
        out_ref[:, h * MLA_V:(h + 1) * MLA_V] = _dot3(o_ref[:, h * KV_RANK:(h + 1) * KV_RANK], w_ref[h])


def _uv_proj(o_lat, w_uv):
    bd = o_lat.shape[0]
    return pl.pallas_call(
        _uv_kernel, out_shape=jax.ShapeDtypeStruct((bd, MLA_WIDTH), F32),
        compiler_params=pltpu.CompilerParams(vmem_limit_bytes=VMEM_LIMIT), name="uv_proj")(o_lat, w_uv)


def _rope_tables(pos, half):
    inv = ROPE_BASE ** (-jnp.arange(half, dtype=F32) / half)
    ang = pos.astype(F32)[:, None] * inv[None, :]
    return jnp.cos(ang), jnp.sin(ang)


def _rope_half_tables(pos):
    cos, sin = _rope_tables(pos, RET_DK // 2)
    return jnp.concatenate([cos, cos], -1), jnp.concatenate([-sin, sin], -1)


def _rope_pad_tables(pos):
    cos, sin = _rope_tables(pos, QK_ROPE // 2)
    z = jnp.zeros_like(cos)
    zz = jnp.zeros((pos.shape[0], LANES - QK_ROPE), F32)
    return (jnp.concatenate([cos, cos, zz], -1), jnp.concatenate([-sin, z, zz], -1),
            jnp.concatenate([z, sin, zz], -1))


def _retention_consts():
    lg = jnp.log1p(-jnp.exp2(-5.0 - jnp.arange(H_RET, dtype=F32)))
    i = jnp.arange(RET_CHUNK, dtype=F32)
    rel = i[:, None] - i[None, :]
    dmat = jnp.where(rel[None] >= 0, jnp.exp(jnp.maximum(rel, 0.0)[None] * lg[:, None, None]), 0.0)
    cdec = jnp.exp(lg[:, None] * (i + 1.0)[None, :])
    kdec = jnp.exp(lg[:, None] * (RET_CHUNK - 1.0 - i)[None, :])
    wide = lambda a: jnp.broadcast_to(a[:, :, None], (H_RET, RET_CHUNK, LANES))
    gc = jnp.broadcast_to(jnp.exp(lg * RET_CHUNK)[:, None, None], (H_RET, 1, LANES))
    g1 = jnp.broadcast_to(jnp.exp(lg)[:, None, None], (H_RET, 1, LANES))
    return (dmat, wide(cdec), wide(kdec), gc), g1


def _layer_weights(l, w_in, w_uq, w_ukv, w_o):
    wi = w_in[l]
    w_r = wi[:, :R_COLS].astype(BF16)
    w_c32 = jnp.pad(wi[:, R_COLS:], ((0, 0), (0, LANES - QK_ROPE)))
    uq = w_uq[l].reshape(Q_RANK, H_MLA, QK_NOPE + QK_ROPE)
    uq32 = jnp.pad(uq, ((0, 0), (0, 0), (0, HEAD_PAD - QK_NOPE - QK_ROPE))).reshape(Q_RANK, H_MLA * HEAD_PAD)
    ukv = w_ukv[l].astype(BF16)
    ukv3 = w_ukv[l].reshape(KV_RANK, H_MLA, QK_NOPE + MLA_V)
    uk_t32 = jnp.transpose(ukv3[..., :QK_NOPE], (1, 2, 0))
    uv32 = jnp.transpose(ukv3[..., QK_NOPE:], (1, 0, 2))
    return dict(w_r=w_r, w_c=w_c32.astype(BF16), w_c32=w_c32, uq=uq32.astype(BF16), uq32=uq32, ukv=ukv,
                uk_t32=uk_t32, uv32=uv32, w_o=w_o[l].astype(BF16))


def kernel(x_prompt, x_sample, cache_ckv, cache_kpe, state_ret, page_table, meta_tokens, ln0_g, ln0_b, w_in, q_norm_g, w_uq, kv_norm_g, w_ukv, ret_gn_g, ret_gn_b, w_o, ln1_g, ln1_b, w_router, router_bias, w1, w3, w2, ln2_g, ln2_b):
    b, seq, d = x_prompt.shape
    bd = x_sample.shape[0]
    t = seq + N_META
    past_len = page_table.shape[1] * PAGE_SIZE
    depth = w_in.shape[0]

    ret_tab_p = _rope_half_tables(jnp.arange(t))
    ret_tab_s = _rope_half_tables(jnp.full((1,), past_len))
    pad_tab_p = _rope_pad_tables(jnp.arange(t))
    pad_tab_s = _rope_pad_tables(jnp.full((bd,), past_len))
    ret_consts, gamma1 = _retention_consts()
    cache_kpe_t = jnp.swapaxes(cache_kpe, 2, 3)
    w1_b, w3_b, w2_b = w1.astype(BF16), w3.astype(BF16), w2.astype(BF16)
    wr_hi = w_router.astype(BF16)
    wr_lo = (w_router - wr_hi.astype(F32)).astype(BF16)

    meta = jnp.broadcast_to(meta_tokens[None], (b, N_META, d))
    xp = jnp.concatenate([meta, x_prompt], 1).reshape(b * t, d)
    hp, hp_b = _ln(xp, ln0_g, ln0_b)
    hs, hs_b = _ln(x_sample.reshape(bd, d), ln0_g, ln0_b)

    outs = [[] for _ in range(6)]
    for l in range(depth):
        w = _layer_weights(l, w_in, w_uq, w_ukv, w_o)
        qg, kg = q_norm_g[l].reshape(1, Q_RANK), kv_norm_g[l].reshape(1, KV_RANK)

        proj_r = _mm(hp_b, w["w_r"], 1024, "proj_r")
        proj_c = _mm(hp_b, w["w_c"], C_COLS + LANES, "proj_c")
        q, k, v, ckv, kpe = _mla_proj(proj_c, *pad_tab_p, qg, kg, w["uq"], w["ukv"])
        ret_o, ret_s = _retention_prompt(proj_r, *ret_tab_p, ret_consts, ret_gn_g[l], ret_gn_b[l], b, t)
        mla_o = _attention_prompt(q, k, v, b, t)
        h1, h1_b, gates = _oproj(ret_o.reshape(b * t, -1), mla_o.reshape(b * t, -1), hp, w["w_o"],
                                 ln1_g[l], ln1_b[l], wr_hi, wr_lo, router_bias)
        hp, hp_b = _res_ln(h1, _moe_sparse(l, h1_b, gates, w1_b, w3_b, w2_b), ln2_g[l], ln2_b[l])
        outs[0].append(ckv.reshape(b, t, KV_RANK))
        outs[1].append(kpe[:, :QK_ROPE].reshape(b, t, QK_ROPE))
        outs[2].append(ret_s)

        proj_r = _mm(hs, w_in, 1024, "proj_r_s", precise=True, layer=l, ncols=R_COLS)
        proj_c = _mm(hs, w["w_c32"], C_COLS + LANES, "proj_c_s", precise=True)
        qlat, qpe, ckv, kpe = _sample_proj(proj_c, *pad_tab_s, qg, kg, w["uq32"], w["uk_t32"])
        ret_o, ret_s = _retention_step(l, proj_r, state_ret, *ret_tab_s, gamma1, ret_gn_g[l], ret_gn_b[l])
        o_lat = _decode_attention(l, page_table, cache_ckv, cache_kpe_t, qlat, qpe, ckv, kpe)
        mla_o = _uv_proj(o_lat, w["uv32"])
        h1, _, gates = _oproj(ret_o, mla_o, hs, w_o, ln1_g[l], ln1_b[l], wr_hi, wr_lo, router_bias, layer=l)
        hs, _ = _res_ln(h1, _moe_dense(l, h1, gates, w1, w3, w2), ln2_g[l], ln2_b[l])
        outs[3].append(ckv.reshape(bd, 1, KV_RANK))
        outs[4].append(kpe[:, :QK_ROPE].reshape(bd, 1, QK_ROPE))
        outs[5].append(ret_s)

    y_prompt = hp.reshape(b, t, d)[:, N_META:]
    y_sample = hs.reshape(bd, 1, d)
    return (y_prompt, y_sample) + tuple(jnp.stack(o, 0) for o in outs)
```

```python
import functools
import math

import jax
import jax.numpy as jnp
from jax import lax
from jax.experimental import pallas as pl
from jax.experimental.pallas import tpu as pltpu

D_MODEL = 2048
N_META = 16
PAGE_SIZE = 128
H_RET = 8
RET_DK = 128
RET_DV = 128
RET_WIDTH = H_RET * RET_DV
RET_CHUNK = 128
H_MLA = 8
MLA_V = 128
MLA_WIDTH = H_MLA * MLA_V
QK_NOPE = 128
QK_ROPE = 64
Q_RANK = 512
KV_RANK = 512
SM_SCALE = (QK_NOPE + QK_ROPE) ** -0.5
N_EXPERTS = 16
N_GROUPS = 4
EXPERTS_PER_GROUP = N_EXPERTS // N_GROUPS
D_EXPERT = 512
ROPE_BASE = 10000.0
NORM_EPS = 1e-5
DEPTH = 2
ALPHA = (2 * DEPTH) ** 0.25
R_COLS = 2 * H_RET * RET_DK + 2 * RET_WIDTH
C_COLS = Q_RANK + KV_RANK
LANES = 128
HEAD_PAD = 2 * LANES
VMEM_LIMIT = 56 * 1024 * 1024
DECODE_PAGES = 16
DECODE_SLOTS = 3
RET_HEADS_PER_STEP = 2
MOE_CHUNK = 128
RET_STEP_SEQS = 4

BF16 = jnp.bfloat16
F32 = jnp.float32


def _tile(n, cap, align=16):
    best = None
    for t in range(align, min(n, cap) + 1, align):
        if n % t == 0:
            best = t
    return best if best is not None else n


def _params(*sem):
    return pltpu.CompilerParams(dimension_semantics=sem, vmem_limit_bytes=VMEM_LIMIT)


def _dot(a, b):
    return jnp.dot(a, b, preferred_element_type=F32)


def _dot_nt(a, b):
    return lax.dot_general(a, b, (((1,), (1,)), ((), ())), preferred_element_type=F32)


def _dot_tn(a, b):
    return lax.dot_general(a, b, (((0,), (0,)), ((), ())), preferred_element_type=F32)


def _split(x):
    hi = x.astype(BF16)
    return hi, (x - hi.astype(F32)).astype(BF16)


def _dot3(a, b, dot=_dot):
    ah, al = _split(a)
    bh, bl = _split(b)
    return dot(ah, bh) + (dot(al, bh) + dot(ah, bl))


def _layer_norm(x, g, b):
    mu = jnp.mean(x, -1, keepdims=True)
    xc = x - mu
    var = jnp.mean(xc * xc, -1, keepdims=True)
    return xc * lax.rsqrt(var + NORM_EPS) * g + b


def _rms_norm(x, g):
    return x * lax.rsqrt(jnp.mean(x * x, -1, keepdims=True) + NORM_EPS) * g


def _silu(x):
    return x * (1.0 / (1.0 + jnp.exp(-x)))


def _rope_half(x, c, s):
    return x * c + pltpu.roll(x, LANES // 2, 1) * s


def _rope_pad(u, c, a, b):
    return u * c + pltpu.roll(u, LANES - QK_ROPE // 2, 1) * a + pltpu.roll(u, QK_ROPE // 2, 1) * b


def _ln_kernel(x_ref, g_ref, b_ref, o_ref, ob_ref):
    y = _layer_norm(x_ref[...], g_ref[...], b_ref[...])
    o_ref[...] = y
    ob_ref[...] = y.astype(BF16)


def _ln(x, g, b):
    n, d = x.shape
    tm = _tile(n, 1024)
    row = pl.BlockSpec((tm, d), lambda i: (i, 0))
    vec = pl.BlockSpec((1, d), lambda i: (0, 0))
    return pl.pallas_call(
        _ln_kernel, grid=(n // tm,), in_specs=[row, vec, vec], out_specs=[row, row],
        out_shape=[jax.ShapeDtypeStruct((n, d), F32), jax.ShapeDtypeStruct((n, d), BF16)],
        compiler_params=_params("parallel"), name="ln0")(x, g.reshape(1, d), b.reshape(1, d))


def _mm_kernel(x_ref, w_ref, o_ref, *, precise):
    x, w = x_ref[...], w_ref[...]
    o_ref[...] = _dot3(x, w, _dot_nt) if precise else _dot_nt(x, w)


def _mm(x, wt, tn, name, precise=False, layer=None, ncols=None):
    n, k = x.shape
    nw = ncols if ncols is not None else wt.shape[-2]
    tm = _tile(n, 1024)
    if layer is None:
        w_spec = pl.BlockSpec((tn, k), lambda j, i: (j, 0))
    else:
        w_spec = pl.BlockSpec((None, tn, k), lambda j, i: (layer, j, 0))
    return pl.pallas_call(
        functools.partial(_mm_kernel, precise=precise), grid=(nw // tn, n // tm),
        in_specs=[pl.BlockSpec((tm, k), lambda j, i: (i, 0)), w_spec],
        out_specs=pl.BlockSpec((tm, tn), lambda j, i: (i, j)),
        out_shape=jax.ShapeDtypeStruct((n, nw), F32),
        compiler_params=_params("parallel", "parallel"), name=name)(x, wt)


def _mla_proj_kernel(x_ref, rc_ref, ra_ref, rb_ref, qg_ref, kg_ref, wuq_ref, wukv_ref,
                     q_ref, k_ref, v_ref, ckv_ref, kpe_ref):
    x = x_ref[...]
    rc, ra, rb = rc_ref[...], ra_ref[...], rb_ref[...]
    cq = _rms_norm(x[:, :Q_RANK], qg_ref[...]).astype(BF16)
    ckv = _rms_norm(x[:, Q_RANK:C_COLS], kg_ref[...])
    kpe = _rope_pad(x[:, C_COLS:], rc, ra, rb)
    ckv_ref[...] = ckv
    kpe_ref[...] = kpe
    kpe_b = kpe.astype(BF16)
    q = _dot(cq, wuq_ref[...])
    kv = _dot(ckv.astype(BF16), wukv_ref[...])
    for h in range(H_MLA):
        lo = h * HEAD_PAD
        q_ref[:, lo:lo + LANES] = q[:, lo:lo + LANES].astype(BF16)
        q_ref[:, lo + LANES:lo + HEAD_PAD] = _rope_pad(q[:, lo + LANES:lo + HEAD_PAD], rc, ra, rb).astype(BF16)
        k_ref[:, lo:lo + LANES] = kv[:, lo:lo + LANES].astype(BF16)
        k_ref[:, lo + LANES:lo + HEAD_PAD] = kpe_b
        v_ref[:, h * MLA_V:(h + 1) * MLA_V] = kv[:, lo + LANES:lo + HEAD_PAD].astype(BF16)


def _mla_proj(proj_c, rope_c, rope_a, rope_b, qg, kg, wuq, wukv):
    n, t = proj_c.shape[0], rope_c.shape[0]
    tm = _tile(t, 1024)
    row = lambda w: pl.BlockSpec((tm, w), lambda i: (i, 0))
    tab = pl.BlockSpec((tm, LANES), lambda i: (i % (t // tm), 0))
    full = lambda a: pl.BlockSpec(a.shape, lambda i: (0,) * a.ndim)
    return pl.pallas_call(
        _mla_proj_kernel, grid=(n // tm,),
        in_specs=[row(C_COLS + LANES), tab, tab, tab, full(qg), full(kg), full(wuq), full(wukv)],
        out_specs=[row(H_MLA * HEAD_PAD), row(H_MLA * HEAD_PAD), row(MLA_WIDTH), row(KV_RANK), row(LANES)],
        out_shape=[jax.ShapeDtypeStruct((n, H_MLA * HEAD_PAD), BF16), jax.ShapeDtypeStruct((n, H_MLA * HEAD_PAD), BF16),
                   jax.ShapeDtypeStruct((n, MLA_WIDTH), BF16), jax.ShapeDtypeStruct((n, KV_RANK), F32),
                   jax.ShapeDtypeStruct((n, LANES), F32)],
        compiler_params=_params("parallel"), name="mla_proj")(proj_c, rope_c, rope_a, rope_b, qg, kg, wuq, wukv)


def _ret_chunk(s, q, k, v, dmat, cdec, kdec, gc):
    qb, vb = q.astype(BF16), v.astype(BF16)
    inner = _dot((_dot_nt(qb, k.astype(BF16)) * dmat).astype(BF16), vb)
    cross = _dot(qb, s.astype(BF16)) * cdec
    s_new = gc * s + _dot_tn((k * kdec).astype(BF16), vb)
    return inner + cross, s_new


def _ret_out(o, g, gn_g, gn_b, dtype=BF16):
    mu = jnp.mean(o, -1, keepdims=True)
    oc = o - mu
    var = jnp.mean(oc * oc, -1, keepdims=True)
    on = oc * lax.rsqrt(var + NORM_EPS) * gn_g + gn_b
    return (_silu(g) * on).astype(dtype)


def _ret_kernel(q_ref, k_ref, v_ref, g_ref, c_ref, s_ref, dmat_ref, cdec_ref, kdec_ref, gc_ref, gng_ref, gnb_ref,
                o_ref, st_ref, pad_ref, *, n_chunks):
    kscale = RET_DK ** -0.5
    heads = range(RET_HEADS_PER_STEP)
    consts = [(dmat_ref[h], cdec_ref[h], kdec_ref[h], gc_ref[h]) for h in heads]

    def load(h, r0, rows):
        sl, ln = pl.ds(r0, rows), slice(h * LANES, (h + 1) * LANES)
        c, s = c_ref[sl, :], s_ref[sl, :]
        return (_rope_half(q_ref[sl, ln], c, s), _rope_half(k_ref[sl, ln], c, s) * kscale, v_ref[sl, ln], g_ref[sl, ln])

    lead = RET_CHUNK - N_META
    pad_ref[...] = jnp.zeros(pad_ref.shape, F32)
    states = []
    for h in heads:
        q0, k0, v0, g0 = load(h, 0, N_META)
        pad_ref[h, 0, lead:, :] = q0
        pad_ref[h, 1, lead:, :] = k0
        pad_ref[h, 2, lead:, :] = v0
        o0, s1 = _ret_chunk(jnp.zeros((RET_DK, RET_DV), F32), pad_ref[h, 0], pad_ref[h, 1], pad_ref[h, 2], *consts[h])
        o_ref[0:N_META, h * LANES:(h + 1) * LANES] = _ret_out(o0[lead:, :], g0, gng_ref[h], gnb_ref[h])
        states.append(s1)

    def body(c, states):
        r0 = pl.multiple_of(N_META + c * RET_CHUNK, 16)
        new = []
        for h in heads:
            q, k, v, g = load(h, r0, RET_CHUNK)
            o, s = _ret_chunk(states[h], q, k, v, *consts[h])
            o_ref[pl.ds(r0, RET_CHUNK), h * LANES:(h + 1) * LANES] = _ret_out(o, g, gng_ref[h], gnb_ref[h])
            new.append(s)
        return tuple(new)

    states = lax.fori_loop(0, n_chunks, body, tuple(states))
    for h in heads:
        st_ref[h] = states[h]


def _retention_prompt(proj_r, cos, sin, consts, gn_g, gn_b, b, t):
    dmat, cdec, kdec, gc = consts
    x = proj_r.reshape(b, t, R_COLS)
    hs, groups = RET_HEADS_PER_STEP, H_RET // RET_HEADS_PER_STEP
    head = lambda part: pl.BlockSpec((None, t, hs * LANES), lambda i, h: (i, 0, part * groups + h))
    tab = pl.BlockSpec((t, LANES), lambda i, h: (0, 0))
    per_h = lambda r: pl.BlockSpec((hs, r, LANES), lambda i, h: (h, 0, 0))
    return pl.pallas_call(
        functools.partial(_ret_kernel, n_chunks=(t - N_META) // RET_CHUNK), grid=(b, groups),
        in_specs=[head(0), head(1), head(2), head(3), tab, tab,
                  per_h(RET_CHUNK), per_h(RET_CHUNK), per_h(RET_CHUNK), per_h(1), per_h(1), per_h(1)],
        out_specs=[pl.BlockSpec((None, t, hs * LANES), lambda i, h: (i, 0, h)),
                   pl.BlockSpec((None, hs, RET_DK, RET_DV), lambda i, h: (i, h, 0, 0))],
        out_shape=[jax.ShapeDtypeStruct((b, t, RET_WIDTH), BF16), jax.ShapeDtypeStruct((b, H_RET, RET_DK, RET_DV), F32)],
        scratch_shapes=[pltpu.VMEM((hs, 3, RET_CHUNK, LANES), F32)],
        compiler_params=_params("parallel", "parallel"), name="retention_prompt",
    )(x, x, x, x, cos, sin, dmat, cdec, kdec, gc, gn_g.reshape(H_RET, 1, RET_DV), gn_b.reshape(H_RET, 1, RET_DV))


def _attn_kernel(q_ref, k_ref, v_ref, o_ref, *, t, blk):
    n_full = t // blk
    starts = [(i * blk, blk) for i in range(n_full)]
    if t % blk:
        starts.append((n_full * blk, t % blk))
    for qi, (q0, qn) in enumerate(starts):
        q = q_ref[q0:q0 + qn, :]
        m = jnp.full((qn, 1), -jnp.inf, F32)
        l = jnp.zeros((qn, 1), F32)
        acc = jnp.zeros((qn, MLA_V), F32)
        for ki, (k0, kn) in enumerate(starts[:qi + 1]):
            s = _dot_nt(q, k_ref[k0:k0 + kn, :]) * SM_SCALE
            if ki == qi:
                rows = lax.broadcasted_iota(jnp.int32, (qn, kn), 0)
                cols = lax.broadcasted_iota(jnp.int32, (qn, kn), 1)
                s = jnp.where(cols <= rows, s, -jnp.inf)
            m_new = jnp.maximum(m, jnp.max(s, -1, keepdims=True))
            p = jnp.exp(s - m_new)
            a = jnp.exp(m - m_new)
            l = a * l + jnp.sum(p, -1, keepdims=True)
            acc = a * acc + _dot(p.astype(BF16), v_ref[k0:k0 + kn, :])
            m = m_new
        o_ref[q0:q0 + qn, :] = (acc / l).astype(BF16)


def _attention_prompt(q, k, v, b, t):
    blk = 512 if t >= 512 else 128
    spec = lambda w: pl.BlockSpec((None, t, w), lambda i, h: (i, 0, h))
    return pl.pallas_call(
        functools.partial(_attn_kernel, t=t, blk=blk), grid=(b, H_MLA),
        in_specs=[spec(HEAD_PAD), spec(HEAD_PAD), spec(MLA_V)], out_specs=spec(MLA_V),
        out_shape=jax.ShapeDtypeStruct((b, t, MLA_WIDTH), BF16),
        compiler_params=_params("parallel", "parallel"), name="attention_prompt",
    )(q.reshape(b, t, -1), k.reshape(b, t, -1), v.reshape(b, t, -1))


def _route(h, wr_hi, wr_lo, bias):
    hi = h.astype(BF16)
    lo = (h - hi.astype(F32)).astype(BF16)
    logits = _dot(hi, wr_hi) + (_dot(lo, wr_hi) + _dot(hi, wr_lo))
    scores = 1.0 / (1.0 + jnp.exp(-logits))
    sel = scores + bias
    lane = lax.broadcasted_iota(jnp.int32, sel.shape, 1).astype(F32)
    neg = -jnp.inf

    def top2(vals):
        t1 = jnp.max(vals, -1, keepdims=True)
        i1 = jnp.min(jnp.where(vals == t1, lane, float(N_EXPERTS)), -1, keepdims=True)
        rest = jnp.where(lane == i1, neg, vals)
        t2 = jnp.max(rest, -1, keepdims=True)
        i2 = jnp.min(jnp.where(rest == t2, lane, float(N_EXPERTS)), -1, keepdims=True)
        return t1, i1, t2, i2

    def in_group(g):
        lo = g * float(EXPERTS_PER_GROUP)
        return jnp.where(lane >= lo, lane, float(N_EXPERTS)) < lo + EXPERTS_PER_GROUP

    best_score = best = None
    for g in range(N_GROUPS):
        t1, _, t2, _ = top2(jnp.where(in_group(float(g)), sel, neg))
        gs = t1 + t2
        if g == 0:
            best_score, best = gs, jnp.zeros_like(gs)
        else:
            upd = gs > best_score
            best_score = jnp.where(upd, gs, best_score)
            best = jnp.where(upd, float(g), best)
    in_best = in_group(best)
    _, i1, _, i2 = top2(jnp.where(in_best, sel, neg))
    s1 = jnp.sum(jnp.where(lane == i1, scores, 0.0), -1, keepdims=True)
    s2 = jnp.sum(jnp.where(lane == i2, scores, 0.0), -1, keepdims=True)
    tot = s1 + s2
    return jnp.where(lane == i1, s1 / tot, 0.0) + jnp.where(lane == i2, s2 / tot, 0.0)


def _oproj_kernel(r_ref, a_ref, h_ref, wa_ref, wb_ref, g_ref, b_ref, whi_ref, wlo_ref, rb_ref,
                  o_ref, ob_ref, gate_ref, *, precise):
    dot = _dot3 if precise else _dot
    y = dot(r_ref[...], wa_ref[...]) + dot(a_ref[...], wb_ref[...])
    h1 = _layer_norm(ALPHA * h_ref[...] + y, g_ref[...], b_ref[...])
    o_ref[...] = h1
    ob_ref[...] = h1.astype(BF16)
    gate_ref[...] = _route(h1, whi_ref[...], wlo_ref[...], rb_ref[...])


def _oproj(ret_o, mla_o, h, w_o, g, b, wr_hi, wr_lo, rbias, layer=None):
    n, d = h.shape
    tm = _tile(n, 512)
    half = w_o.shape[-2] // 2
    row = lambda w: pl.BlockSpec((tm, w), lambda i: (i, 0))
    vec = lambda w: pl.BlockSpec((1, w), lambda i: (0, 0))
    if layer is None:
        wspec = lambda j: pl.BlockSpec((half, d), lambda i: (j, 0))
    else:
        wspec = lambda j: pl.BlockSpec((None, half, d), lambda i: (layer, j, 0))
    rspec = pl.BlockSpec((d, N_EXPERTS), lambda i: (0, 0))
    return pl.pallas_call(
        functools.partial(_oproj_kernel, precise=layer is not None), grid=(n // tm,),
        in_specs=[row(half), row(half), row(d), wspec(0), wspec(1), vec(d), vec(d), rspec, rspec, vec(N_EXPERTS)],
        out_specs=[row(d), row(d), row(N_EXPERTS)],
        out_shape=[jax.ShapeDtypeStruct((n, d), F32), jax.ShapeDtypeStruct((n, d), BF16),
                   jax.ShapeDtypeStruct((n, N_EXPERTS), F32)],
        compiler_params=_params("parallel"), name="oproj_ln1_router",
    )(ret_o, mla_o, h, w_o, w_o, g.reshape(1, d), b.reshape(1, d), wr_hi, wr_lo, rbias.reshape(1, N_EXPERTS))


def _gate_column(gates, e):
    lane = lax.broadcasted_iota(jnp.int32, gates.shape, 1)
    return jnp.sum(jnp.where(lane == e, gates, 0.0), -1, keepdims=True)


def _moe_dense_kernel(x_ref, gate_ref, w1_ref, w3_ref, w2_ref, o_ref, acc_ref):
    e = pl.program_id(1)

    @pl.when(e == 0)
    def _():
        acc_ref[...] = jnp.zeros(acc_ref.shape, F32)

    x = x_ref[...]
    hid = _silu(_dot3(x, w1_ref[...])) * _dot3(x, w3_ref[...])
    acc_ref[...] += _dot3(hid * _gate_column(gate_ref[...], e), w2_ref[...])

    @pl.when(e == N_EXPERTS - 1)
    def _():
        o_ref[...] = acc_ref[...]


def _moe_dense(layer, x, gates, w1, w3, w2):
    n, d = x.shape
    tm = _tile(n, 1024)
    row = lambda w: pl.BlockSpec((tm, w), lambda i, e: (i, 0))
    wspec = lambda r, c: pl.BlockSpec((None, None, r, c), lambda i, e: (layer, e, 0, 0))
    return pl.pallas_call(
        _moe_dense_kernel, grid=(n // tm, N_EXPERTS),
        in_specs=[row(d), row(N_EXPERTS), wspec(d, D_EXPERT), wspec(d, D_EXPERT), wspec(D_EXPERT, d)],
        out_specs=row(d), out_shape=jax.ShapeDtypeStruct((n, d), F32),
        scratch_shapes=[pltpu.VMEM((tm, d), F32)],
        compiler_params=_params("parallel", "arbitrary"), name="moe_dense")(x, gates, w1, w3, w2)


def _moe_sparse_kernel(cnt_ref, x_ref, gate_ref, tri_ref, w1_ref, w3_ref, w2_ref, o_ref, acc_ref, rank_ref):
    i, e = pl.program_id(0), pl.program_id(1)
    tm = x_ref.shape[0]

    @pl.when(e == 0)
    def _():
        acc_ref[...] = jnp.zeros(acc_ref.shape, F32)
        used = jnp.where(gate_ref[...] != 0.0, 1.0, 0.0).astype(BF16)
        rank_ref[...] = _dot(tri_ref[...], used)

    count = cnt_ref[i * N_EXPERTS + e]

    @pl.when(count > 0)
    def _():
        gate = _gate_column(gate_ref[...], e)
        rank = _gate_column(rank_ref[...], e)
        slot = lax.broadcasted_iota(jnp.int32, (1, MOE_CHUNK), 1).astype(F32)

        def chunk(c, carry):
            base = (c * MOE_CHUNK).astype(F32)
            pick = jnp.where((rank == slot + base) & (gate != 0.0), 1.0, 0.0).astype(BF16)
            xs = _dot_tn(pick, x_ref[...]).astype(BF16)
            hid = _silu(_dot(xs, w1_ref[...])) * _dot(xs, w3_ref[...])
            y = _dot(hid.astype(BF16), w2_ref[...])
            acc_ref[...] += gate * _dot(pick, y.astype(BF16))
            return carry

        lax.fori_loop(0, (count + MOE_CHUNK - 1) // MOE_CHUNK, chunk, 0)

    @pl.when(e == N_EXPERTS - 1)
    def _():
        o_ref[...] = acc_ref[...]


def _moe_sparse(layer, xb, gates, w1, w3, w2):
    n, d = xb.shape
    tm = _tile(n, 1024)
    counts = jnp.sum((gates != 0.0).reshape(n // tm, tm, N_EXPERTS), 1, dtype=jnp.int32).reshape(-1)
    tri = jnp.tril(jnp.ones((tm, tm), BF16), -1)
    row = lambda w: pl.BlockSpec((tm, w), lambda i, e, c: (i, 0))
    wspec = lambda r, c: pl.BlockSpec((None, None, r, c), lambda i, e, cnt: (layer, e, 0, 0))
    grid_spec = pltpu.PrefetchScalarGridSpec(
        num_scalar_prefetch=1, grid=(n // tm, N_EXPERTS),
        in_specs=[row(d), row(N_EXPERTS), pl.BlockSpec((tm, tm), lambda i, e, c: (0, 0)),
                  wspec(d, D_EXPERT), wspec(d, D_EXPERT), wspec(D_EXPERT, d)],
        out_specs=row(d),
        scratch_shapes=[pltpu.VMEM((tm, d), F32), pltpu.VMEM((tm, N_EXPERTS), F32)])
    return pl.pallas_call(
        _moe_sparse_kernel, grid_spec=grid_spec, out_shape=jax.ShapeDtypeStruct((n, d), F32),
        compiler_params=_params("parallel", "arbitrary"), name="moe_sparse")(counts, xb, gates, tri, w1, w3, w2)


def _res_ln_kernel(h_ref, y_ref, g_ref, b_ref, o_ref, ob_ref):
    z = _layer_norm(ALPHA * h_ref[...] + y_ref[...], g_ref[...], b_ref[...])
    o_ref[...] = z
    ob_ref[...] = z.astype(BF16)


def _res_ln(h, y, g, b):
    n, d = h.shape
    tm = _tile(n, 1024)
    row = pl.BlockSpec((tm, d), lambda i: (i, 0))
    vec = pl.BlockSpec((1, d), lambda i: (0, 0))
    return pl.pallas_call(
        _res_ln_kernel, grid=(n // tm,), in_specs=[row, row, vec, vec], out_specs=[row, row],
        out_shape=[jax.ShapeDtypeStruct((n, d), F32), jax.ShapeDtypeStruct((n, d), BF16)],
        compiler_params=_params("parallel"), name="residual_ln2")(h, y, g.reshape(1, d), b.reshape(1, d))


def _sample_proj_kernel(x_ref, rc_ref, ra_ref, rb_ref, qg_ref, kg_ref, wuq_ref, wuk_ref,
                        qlat_ref, qpe_ref, ckv_ref, kpe_ref):
    x = x_ref[...]
    rc, ra, rb = rc_ref[...], ra_ref[...], rb_ref[...]
    cq = _rms_norm(x[:, :Q_RANK], qg_ref[...])
    ckv_ref[...] = _rms_norm(x[:, Q_RANK:C_COLS], kg_ref[...])
    kpe_ref[...] = _rope_pad(x[:, C_COLS:], rc, ra, rb)
    q = _dot3(cq, wuq_ref[...])
    for h in range(H_MLA):
        lo = h * HEAD_PAD
        qlat_ref[:, h * KV_RANK:(h + 1) * KV_RANK] = _dot3(q[:, lo:lo + LANES], wuk_ref[h]).astype(BF16)
        qpe_ref[:, h * LANES:(h + 1) * LANES] = _rope_pad(q[:, lo + LANES:lo + HEAD_PAD], rc, ra, rb).astype(BF16)


def _sample_proj(proj_c, rope_c, rope_a, rope_b, qg, kg, wuq, wuk_t):
    n = proj_c.shape[0]
    return pl.pallas_call(
        _sample_proj_kernel,
        out_shape=[jax.ShapeDtypeStruct((n, H_MLA * KV_RANK), BF16), jax.ShapeDtypeStruct((n, H_MLA * LANES), BF16),
                   jax.ShapeDtypeStruct((n, KV_RANK), F32), jax.ShapeDtypeStruct((n, LANES), F32)],
        compiler_params=pltpu.CompilerParams(vmem_limit_bytes=VMEM_LIMIT), name="sample_proj",
    )(proj_c, rope_c, rope_a, rope_b, qg, kg, wuq, wuk_t)


def _ret_step_kernel(x_ref, s_ref, c_ref, sn_ref, gc_ref, gng_ref, gnb_ref, o_ref, st_ref):
    c, sn = c_ref[...], sn_ref[...]
    kscale = RET_DK ** -0.5
    w = H_RET * RET_DK
    rows = 16
    row = lax.broadcasted_iota(jnp.int32, (rows, RET_DK), 0)
    for b in range(x_ref.shape[0]):
        x = x_ref[b]
        for h in range(H_RET):
            q = _rope_half(x[:, h * RET_DK:(h + 1) * RET_DK], c, sn)
            k = _rope_half(x[:, w + h * RET_DK:w + (h + 1) * RET_DK], c, sn) * kscale
            v = x[:, 2 * w + h * RET_DV:2 * w + (h + 1) * RET_DV]
            g = x[:, 3 * w + h * RET_DV:3 * w + (h + 1) * RET_DV]
            k0 = jnp.where(row == 0, k, 0.0)
            s_new = gc_ref[h] * s_ref[b, h] + _dot3(k0, jnp.broadcast_to(v, (rows, RET_DV)), _dot_tn)
            st_ref[b, h] = s_new
            o = _dot3(jnp.broadcast_to(q, (rows, RET_DK)), s_new)[:1, :]
            o_ref[b, :, h * RET_DV:(h + 1) * RET_DV] = _ret_out(o, g, gng_ref[h], gnb_ref[h], F32)


def _retention_step(layer, proj_r, state, cos, sin, gc, gn_g, gn_b):
    bd = proj_r.shape[0]
    nb = _tile(bd, RET_STEP_SEQS, 1)
    full = lambda a: pl.BlockSpec(a.shape, lambda i: (0,) * a.ndim)
    gng, gnb = gn_g.reshape(H_RET, 1, RET_DV), gn_b.reshape(H_RET, 1, RET_DV)
    o, st = pl.pallas_call(
        _ret_step_kernel, grid=(bd // nb,),
        in_specs=[pl.BlockSpec((nb, 1, R_COLS), lambda i: (i, 0, 0)),
                  pl.BlockSpec((None, nb, H_RET, RET_DK, RET_DV), lambda i: (layer, i, 0, 0, 0)),
                  full(cos), full(sin), full(gc), full(gng), full(gnb)],
        out_specs=[pl.BlockSpec((nb, 1, RET_WIDTH), lambda i: (i, 0, 0)),
                   pl.BlockSpec((nb, H_RET, RET_DK, RET_DV), lambda i: (i, 0, 0, 0))],
        out_shape=[jax.ShapeDtypeStruct((bd, 1, RET_WIDTH), F32), jax.ShapeDtypeStruct(state.shape[1:], F32)],
        compiler_params=_params("parallel"), name="retention_step",
    )(proj_r.reshape(bd, 1, R_COLS), state, cos, sin, gc, gng, gnb)
    return o.reshape(bd, RET_WIDTH), st


def _decode_kernel(pt_ref, qlat_ref, qpe_ref, cnew_ref, knew_ref, ckv_hbm, kpe_hbm, o_ref,
                   m_ref, l_ref, acc_ref, cbuf_ref, kbuf_ref, sem, *, layer, pages, n_groups, n_steps):
    i, j = pl.program_id(0), pl.program_id(1)
    step = i * n_groups + j
    slot = step % DECODE_SLOTS
    ahead = DECODE_SLOTS - 1

    def page_copies(seq, group, s):
        copies = []
        for p in range(pages):
            page = pt_ref[seq, group * pages + p]
            copies.append(pltpu.make_async_copy(ckv_hbm.at[layer, page], cbuf_ref.at[s, p], sem.at[s]))
            copies.append(pltpu.make_async_copy(kpe_hbm.at[layer, page], kbuf_ref.at[s, p], sem.at[s]))
        return copies

    @pl.when(step == 0)
    def _():
        for first in range(min(ahead, n_steps)):
            for c in page_copies(first // n_groups, first % n_groups, first):
                c.start()

    later = step + ahead

    @pl.when(later < n_steps)
    def _():
        for c in page_copies(later // n_groups, later % n_groups, later % DECODE_SLOTS):
            c.start()

    for c in page_copies(i, j, slot):
        c.wait()

    @pl.when(j == 0)
    def _():
        m_ref[...] = jnp.full(m_ref.shape, -jnp.inf, F32)
        l_ref[...] = jnp.zeros(l_ref.shape, F32)
        acc_ref[...] = jnp.zeros(acc_ref.shape, F32)

    qlat = qlat_ref[...]
    qpe = qpe_ref[...][:, :QK_ROPE]
    qlat32, qpe32 = qlat.astype(F32), qpe.astype(F32)

    s = jnp.concatenate([_dot_nt(qlat32, cbuf_ref[slot, p]) + _dot(qpe32, kbuf_ref[slot, p])
                         for p in range(pages)], -1) * SM_SCALE
    m_old = m_ref[...]
    m_new = jnp.maximum(m_old, jnp.max(s, -1, keepdims=True))
    p_all = jnp.exp(s - m_new)
    a = jnp.exp(m_old - m_new)
    l_ref[...] = a * l_ref[...] + jnp.sum(p_all, -1, keepdims=True)
    pv = _dot(p_all[:, :PAGE_SIZE], cbuf_ref[slot, 0])
    for p in range(1, pages):
        pv += _dot(p_all[:, p * PAGE_SIZE:(p + 1) * PAGE_SIZE], cbuf_ref[slot, p])
    acc_ref[...] = a * acc_ref[...] + pv
    m_ref[...] = m_new

    @pl.when(j == pl.num_programs(1) - 1)
    def _():
        cn = jnp.broadcast_to(cnew_ref[...], (8, KV_RANK)).astype(BF16)
        kn = jnp.broadcast_to(knew_ref[...][:, :QK_ROPE], (8, QK_ROPE)).astype(BF16)
        s = (_dot_nt(qlat, cn) + _dot_nt(qpe, kn))[:, :1] * SM_SCALE
        m_old = m_ref[...]
        m_new = jnp.maximum(m_old, s)
        p = jnp.exp(s - m_new)
        a = jnp.exp(m_old - m_new)
        l = a * l_ref[...] + p
        acc = a * acc_ref[...] + p.astype(BF16).astype(F32) * cn[:1, :].astype(F32)
        o = acc / l
        for h in range(H_MLA):
            o_ref[:, h * KV_RANK:(h + 1) * KV_RANK] = o[h:h + 1, :]


def _decode_attention(layer, page_table, cache_ckv, cache_kpe_t, qlat, qpe, ckv_new, kpe_new):
    bd, n_pages = page_table.shape
    pages = _tile(n_pages, DECODE_PAGES, 1)
    per_b = lambda r, w: pl.BlockSpec((None, r, w), lambda i, j, pt: (i, 0, 0))
    in_hbm = pl.BlockSpec(memory_space=pl.ANY)
    grid_spec = pltpu.PrefetchScalarGridSpec(
        num_scalar_prefetch=1, grid=(bd, n_pages // pages),
        in_specs=[per_b(H_MLA, KV_RANK), per_b(H_MLA, LANES), per_b(1, KV_RANK), per_b(1, LANES), in_hbm, in_hbm],
        out_specs=per_b(1, H_MLA * KV_RANK),
        scratch_shapes=[pltpu.VMEM((H_MLA, 1), F32), pltpu.VMEM((H_MLA, 1), F32), pltpu.VMEM((H_MLA, KV_RANK), F32),
                        pltpu.VMEM((DECODE_SLOTS, pages, PAGE_SIZE, KV_RANK), F32),
                        pltpu.VMEM((DECODE_SLOTS, pages, QK_ROPE, PAGE_SIZE), F32),
                        pltpu.SemaphoreType.DMA((DECODE_SLOTS,))])
    n_groups = n_pages // pages
    o = pl.pallas_call(
        functools.partial(_decode_kernel, layer=layer, pages=pages, n_groups=n_groups, n_steps=bd * n_groups),
        grid_spec=grid_spec,
        out_shape=jax.ShapeDtypeStruct((bd, 1, H_MLA * KV_RANK), F32),
        compiler_params=_params("arbitrary", "arbitrary"), name="decode_attention",
    )(page_table, qlat.reshape(bd, H_MLA, KV_RANK), qpe.reshape(bd, H_MLA, LANES),
      ckv_new.reshape(bd, 1, KV_RANK), kpe_new.reshape(bd, 1, LANES), cache_ckv, cache_kpe_t)
    return o.reshape(bd, H_MLA * KV_RANK)


def _uv_kernel(o_ref, w_ref, out_ref):
    for h in range(H_MLA):
        out_ref[:, h * MLA_V:(h + 1) * MLA_V] = _dot3(o_ref[:, h * KV_RANK:(h + 1) * KV_RANK], w_ref[h])


def _uv_proj(o_lat, w_uv):
    bd = o_lat.shape[0]
    return pl.pallas_call(
        _uv_kernel, out_shape=jax.ShapeDtypeStruct((bd, MLA_WIDTH), F32),
        compiler_params=pltpu.CompilerParams(vmem_limit_bytes=VMEM_LIMIT), name="uv_proj")(o_lat, w_uv)


def _rope_tables(pos, half):
    inv = ROPE_BASE ** (-jnp.arange(half, dtype=F32) / half)
    ang = pos.astype(F32)[:, None] * inv[None, :]
    return jnp.cos(ang), jnp.sin(ang)


def _rope_half_tables(pos):
    cos, sin = _rope_tables(pos, RET_DK // 2)
    return jnp.concatenate([cos, cos], -1), jnp.concatenate([-sin, sin], -1)


def _rope_pad_tables(pos):
    cos, sin = _rope_tables(pos, QK_ROPE // 2)
    z = jnp.zeros_like(cos)
    zz = jnp.zeros((pos.shape[0], LANES - QK_ROPE), F32)
    return (jnp.concatenate([cos, cos, zz], -1), jnp.concatenate([-sin, z, zz], -1),
            jnp.concatenate([z, sin, zz], -1))


def _retention_consts():
    lg = jnp.log1p(-jnp.exp2(-5.0 - jnp.arange(H_RET, dtype=F32)))
    i = jnp.arange(RET_CHUNK, dtype=F32)
    rel = i[:, None] - i[None, :]
    dmat = jnp.where(rel[None] >= 0, jnp.exp(jnp.maximum(rel, 0.0)[None] * lg[:, None, None]), 0.0)
    cdec = jnp.exp(lg[:, None] * (i + 1.0)[None, :])
    kdec = jnp.exp(lg[:, None] * (RET_CHUNK - 1.0 - i)[None, :])
    wide = lambda a: jnp.broadcast_to(a[:, :, None], (H_RET, RET_CHUNK, LANES))
    gc = jnp.broadcast_to(jnp.exp(lg * RET_CHUNK)[:, None, None], (H_RET, 1, LANES))
    g1 = jnp.broadcast_to(jnp.exp(lg)[:, None, None], (H_RET, 1, LANES))
    return (dmat, wide(cdec), wide(kdec), gc), g1


def _layer_weights(l, w_in_t, w_uq, w_ukv, w_o):
    w_c32 = jnp.pad(w_in_t[l, R_COLS:], ((0, LANES - QK_ROPE), (0, 0)))
    uq = w_uq[l].reshape(Q_RANK, H_MLA, QK_NOPE + QK_ROPE)
    uq32 = jnp.pad(uq, ((0, 0), (0, 0), (0, HEAD_PAD - QK_NOPE - QK_ROPE))).reshape(Q_RANK, H_MLA * HEAD_PAD)
    ukv = w_ukv[l].astype(BF16)
    ukv3 = w_ukv[l].reshape(KV_RANK, H_MLA, QK_NOPE + MLA_V)
    uk_t32 = jnp.transpose(ukv3[..., :QK_NOPE], (1, 2, 0))
    uv32 = jnp.transpose(ukv3[..., QK_NOPE:], (1, 0, 2))
    return dict(w_c=w_c32.astype(BF16), w_c32=w_c32, uq=uq32.astype(BF16), uq32=uq32, ukv=ukv,
                uk_t32=uk_t32, uv32=uv32, w_o=w_o[l].astype(BF16))


def kernel(x_prompt, x_sample, cache_ckv, cache_kpe, state_ret, page_table, meta_tokens, ln0_g, ln0_b, w_in, q_norm_g, w_uq, kv_norm_g, w_ukv, ret_gn_g, ret_gn_b, w_o, ln1_g, ln1_b, w_router, router_bias, w1, w3, w2, ln2_g, ln2_b):
    b, seq, d = x_prompt.shape
    bd = x_sample.shape[0]
    t = seq + N_META
    past_len = page_table.shape[1] * PAGE_SIZE
    depth = w_in.shape[0]

    ret_tab_p = _rope_half_tables(jnp.arange(t))
    ret_tab_s = _rope_half_tables(jnp.full((1,), past_len))
    pad_tab_p = _rope_pad_tables(jnp.arange(t))
    pad_tab_s = _rope_pad_tables(jnp.full((bd,), past_len))
    ret_consts, gamma1 = _retention_consts()
    cache_kpe_t = jnp.swapaxes(cache_kpe, 2, 3)
    w_in_t = jnp.swapaxes(w_in, 1, 2)
    w_in_tb = w_in_t.astype(BF16)
    w1_b, w3_b, w2_b = w1.astype(BF16), w3.astype(BF16), w2.astype(BF16)
    wr_hi = w_router.astype(BF16)
    wr_lo = (w_router - wr_hi.astype(F32)).astype(BF16)

    meta = jnp.broadcast_to(meta_tokens[None], (b, N_META, d))
    xp = jnp.concatenate([meta, x_prompt], 1).reshape(b * t, d)
    hp, hp_b = _ln(xp, ln0_g, ln0_b)
    hs, hs_b = _ln(x_sample.reshape(bd, d), ln0_g, ln0_b)

    outs = [[] for _ in range(6)]
    for l in range(depth):
        w = _layer_weights(l, w_in_t, w_uq, w_ukv, w_o)
        qg, kg = q_norm_g[l].reshape(1, Q_RANK), kv_norm_g[l].reshape(1, KV_RANK)

        proj_r = _mm(hp_b, w_in_tb, 1024, "proj_r", layer=l, ncols=R_COLS)
        proj_c = _mm(hp_b, w["w_c"], C_COLS + LANES, "proj_c")
        q, k, v, ckv, kpe = _mla_proj(proj_c, *pad_tab_p, qg, kg, w["uq"], w["ukv"])
        ret_o, ret_s = _retention_prompt(proj_r, *ret_tab_p, ret_consts, ret_gn_g[l], ret_gn_b[l], b, t)
        mla_o = _attention_prompt(q, k, v, b, t)
        h1, h1_b, gates = _oproj(ret_o.reshape(b * t, -1), mla_o.reshape(b * t, -1), hp, w["w_o"],
                                 ln1_g[l], ln1_b[l], wr_hi, wr_lo, router_bias)
        hp, hp_b = _res_ln(h1, _moe_sparse(l, h1_b, gates, w1_b, w3_b, w2_b), ln2_g[l], ln2_b[l])
        outs[0].append(ckv.reshape(b, t, KV_RANK))
        outs[1].append(kpe[:, :QK_ROPE].reshape(b, t, QK_ROPE))
        outs[2].append(ret_s)

        proj_r = _mm(hs, w_in_t, 1024, "proj_r_s", precise=True, layer=l, ncols=R_COLS)
        proj_c = _mm(hs, w["w_c32"], C_COLS + LANES, "proj_c_s", precise=True)
        qlat, qpe, ckv, kpe = _sample_proj(proj_c, *pad_tab_s, qg, kg, w["uq32"], w["uk_t32"])
        ret_o, ret_s = _retention_step(l, proj_r, state_ret, *ret_tab_s, gamma1, ret_gn_g[l], ret_gn_b[l])
        o_lat = _decode_attention(l, page_table, cache_ckv, cache_kpe_t, qlat, qpe, ckv, kpe)
        mla_o = _uv_proj(o_lat, w["uv32"])
        h1, _, gates = _oproj(ret_o, mla_o, hs, w_o, ln1_g[l], ln1_b[l], wr_hi, wr_lo, router_bias, layer=l)
        hs, _ = _res_ln(h1, _moe_dense(l, h1, gates, w1, w3, w2), ln2_g[l], ln2_b[l])
        outs[3].append(ckv.reshape(bd, 1, KV_RANK))
        outs[4].append(kpe[:, :QK_ROPE].reshape(bd, 1, QK_ROPE))
        outs[5].append(ret_s)

    y_prompt = hp.reshape(b, t, d)[:, N_META:]
    y_sample = hs.reshape(bd, 1, d)
    return (y_prompt, y_sample) + tuple(jnp.stack(o, 0) for o in outs)
```

```python
import functools
import math

import jax
import jax.numpy as jnp
from jax import lax
from jax.experimental import pallas as pl
from jax.experimental.pallas import tpu as pltpu

D_MODEL = 2048
N_META = 16
PAGE_SIZE = 128
H_RET = 8
RET_DK = 128
RET_DV = 128
RET_WIDTH = H_RET * RET_DV
RET_CHUNK = 128
H_MLA = 8
MLA_V = 128
MLA_WIDTH = H_MLA * MLA_V
QK_NOPE = 128
QK_ROPE = 64
Q_RANK = 512
KV_RANK = 512
SM_SCALE = (QK_NOPE + QK_ROPE) ** -0.5
N_EXPERTS = 16
N_GROUPS = 4
EXPERTS_PER_GROUP = N_EXPERTS // N_GROUPS
D_EXPERT = 512
ROPE_BASE = 10000.0
NORM_EPS = 1e-5
DEPTH = 2
ALPHA = (2 * DEPTH) ** 0.25
R_COLS = 2 * H_RET * RET_DK + 2 * RET_WIDTH
C_COLS = Q_RANK + KV_RANK
LANES = 128
HEAD_PAD = 2 * LANES
VMEM_LIMIT = 56 * 1024 * 1024
DECODE_PAGES = 16
DECODE_SLOTS = 4
RET_HEADS_PER_STEP = 2
MOE_CHUNK = 128
RET_STEP_SEQS = 8

BF16 = jnp.bfloat16
F32 = jnp.float32


def _tile(n, cap, align=16):
    best = None
    for t in range(align, min(n, cap) + 1, align):
        if n % t == 0:
            best = t
    return best if best is not None else n


def _params(*sem):
    return pltpu.CompilerParams(dimension_semantics=sem, vmem_limit_bytes=VMEM_LIMIT)


def _dot(a, b):
    return jnp.dot(a, b, preferred_element_type=F32)


def _dot_nt(a, b):
    return lax.dot_general(a, b, (((1,), (1,)), ((), ())), preferred_element_type=F32)


def _dot_tn(a, b):
    return lax.dot_general(a, b, (((0,), (0,)), ((), ())), preferred_element_type=F32)


def _split(x):
    hi = x.astype(BF16)
    return hi, (x - hi.astype(F32)).astype(BF16)


def _dot3(a, b, dot=_dot):
    ah, al = _split(a)
    bh, bl = _split(b)
    return dot(ah, bh) + (dot(al, bh) + dot(ah, bl))


def _layer_norm(x, g, b):
    mu = jnp.mean(x, -1, keepdims=True)
    xc = x - mu
    var = jnp.mean(xc * xc, -1, keepdims=True)
    return xc * lax.rsqrt(var + NORM_EPS) * g + b


def _rms_norm(x, g):
    return x * lax.rsqrt(jnp.mean(x * x, -1, keepdims=True) + NORM_EPS) * g


def _silu(x):
    return x * (1.0 / (1.0 + jnp.exp(-x)))


def _rope_half(x, c, s):
    return x * c + pltpu.roll(x, LANES // 2, 1) * s


def _rope_pad(u, c, a, b):
    return u * c + pltpu.roll(u, LANES - QK_ROPE // 2, 1) * a + pltpu.roll(u, QK_ROPE // 2, 1) * b


def _ln_kernel(x_ref, g_ref, b_ref, o_ref, ob_ref):
    y = _layer_norm(x_ref[...], g_ref[...], b_ref[...])
    o_ref[...] = y
    ob_ref[...] = y.astype(BF16)


def _ln(x, g, b):
    n, d = x.shape
    tm = _tile(n, 1024)
    row = pl.BlockSpec((tm, d), lambda i: (i, 0))
    vec = pl.BlockSpec((1, d), lambda i: (0, 0))
    return pl.pallas_call(
        _ln_kernel, grid=(n // tm,), in_specs=[row, vec, vec], out_specs=[row, row],
        out_shape=[jax.ShapeDtypeStruct((n, d), F32), jax.ShapeDtypeStruct((n, d), BF16)],
        compiler_params=_params("parallel"), name="ln0")(x, g.reshape(1, d), b.reshape(1, d))


def _mm_kernel(x_ref, w_ref, o_ref, *, precise):
    x, w = x_ref[...], w_ref[...]
    o_ref[...] = _dot3(x, w, _dot_nt) if precise else _dot_nt(x, w)


def _mm(x, wt, tn, name, precise=False, layer=None, ncols=None):
    n, k = x.shape
    nw = ncols if ncols is not None else wt.shape[-2]
    tm = _tile(n, 1024)
    if layer is None:
        w_spec = pl.BlockSpec((tn, k), lambda j, i: (j, 0))
    else:
        w_spec = pl.BlockSpec((None, tn, k), lambda j, i: (layer, j, 0))
    return pl.pallas_call(
        functools.partial(_mm_kernel, precise=precise), grid=(nw // tn, n // tm),
        in_specs=[pl.BlockSpec((tm, k), lambda j, i: (i, 0)), w_spec],
        out_specs=pl.BlockSpec((tm, tn), lambda j, i: (i, j)),
        out_shape=jax.ShapeDtypeStruct((n, nw), F32),
        compiler_params=_params("parallel", "parallel"), name=name)(x, wt)


def _mla_proj_kernel(x_ref, rc_ref, ra_ref, rb_ref, qg_ref, kg_ref, wuq_ref, wukv_ref,
                     q_ref, k_ref, v_ref, ckv_ref, kpe_ref):
    x = x_ref[...]
    rc, ra, rb = rc_ref[...], ra_ref[...], rb_ref[...]
    cq = _rms_norm(x[:, :Q_RANK], qg_ref[...]).astype(BF16)
    ckv = _rms_norm(x[:, Q_RANK:C_COLS], kg_ref[...])
    kpe = _rope_pad(x[:, C_COLS:], rc, ra, rb)
    ckv_ref[...] = ckv
    kpe_ref[...] = kpe
    kpe_b = kpe.astype(BF16)
    q = _dot(cq, wuq_ref[...])
    kv = _dot(ckv.astype(BF16), wukv_ref[...])
    for h in range(H_MLA):
        lo = h * HEAD_PAD
        q_ref[:, lo:lo + LANES] = q[:, lo:lo + LANES].astype(BF16)
        q_ref[:, lo + LANES:lo + HEAD_PAD] = _rope_pad(q[:, lo + LANES:lo + HEAD_PAD], rc, ra, rb).astype(BF16)
        k_ref[:, lo:lo + LANES] = kv[:, lo:lo + LANES].astype(BF16)
        k_ref[:, lo + LANES:lo + HEAD_PAD] = kpe_b
        v_ref[:, h * MLA_V:(h + 1) * MLA_V] = kv[:, lo + LANES:lo + HEAD_PAD].astype(BF16)


def _mla_proj(proj_c, rope_c, rope_a, rope_b, qg, kg, wuq, wukv):
    n, t = proj_c.shape[0], rope_c.shape[0]
    tm = _tile(t, 1024)
    row = lambda w: pl.BlockSpec((tm, w), lambda i: (i, 0))
    tab = pl.BlockSpec((tm, LANES), lambda i: (i % (t // tm), 0))
    full = lambda a: pl.BlockSpec(a.shape, lambda i: (0,) * a.ndim)
    return pl.pallas_call(
        _mla_proj_kernel, grid=(n // tm,),
        in_specs=[row(C_COLS + LANES), tab, tab, tab, full(qg), full(kg), full(wuq), full(wukv)],
        out_specs=[row(H_MLA * HEAD_PAD), row(H_MLA * HEAD_PAD), row(MLA_WIDTH), row(KV_RANK), row(LANES)],
        out_shape=[jax.ShapeDtypeStruct((n, H_MLA * HEAD_PAD), BF16), jax.ShapeDtypeStruct((n, H_MLA * HEAD_PAD), BF16),
                   jax.ShapeDtypeStruct((n, MLA_WIDTH), BF16), jax.ShapeDtypeStruct((n, KV_RANK), F32),
                   jax.ShapeDtypeStruct((n, LANES), F32)],
        compiler_params=_params("parallel"), name="mla_proj")(proj_c, rope_c, rope_a, rope_b, qg, kg, wuq, wukv)


def _ret_chunk(s, q, k, v, dmat, cdec, kdec, gc):
    qb, vb = q.astype(BF16), v.astype(BF16)
    inner = _dot((_dot_nt(qb, k.astype(BF16)) * dmat).astype(BF16), vb)
    cross = _dot(qb, s.astype(BF16)) * cdec
    s_new = gc * s + _dot_tn((k * kdec).astype(BF16), vb)
    return inner + cross, s_new


def _ret_out(o, g, gn_g, gn_b, dtype=BF16):
    mu = jnp.mean(o, -1, keepdims=True)
    oc = o - mu
    var = jnp.mean(oc * oc, -1, keepdims=True)
    on = oc * lax.rsqrt(var + NORM_EPS) * gn_g + gn_b
    return (_silu(g) * on).astype(dtype)


def _ret_kernel(q_ref, k_ref, v_ref, g_ref, c_ref, s_ref, dmat_ref, cdec_ref, kdec_ref, gc_ref, gng_ref, gnb_ref,
                o_ref, st_ref, pad_ref, *, n_chunks):
    kscale = RET_DK ** -0.5
    heads = range(RET_HEADS_PER_STEP)
    consts = [(dmat_ref[h], cdec_ref[h], kdec_ref[h], gc_ref[h]) for h in heads]

    def load(h, r0, rows):
        sl, ln = pl.ds(r0, rows), slice(h * LANES, (h + 1) * LANES)
        c, s = c_ref[sl, :], s_ref[sl, :]
        return (_rope_half(q_ref[sl, ln], c, s), _rope_half(k_ref[sl, ln], c, s) * kscale, v_ref[sl, ln], g_ref[sl, ln])

    lead = RET_CHUNK - N_META
    pad_ref[...] = jnp.zeros(pad_ref.shape, F32)
    states = []
    for h in heads:
        q0, k0, v0, g0 = load(h, 0, N_META)
        pad_ref[h, 0, lead:, :] = q0
        pad_ref[h, 1, lead:, :] = k0
        pad_ref[h, 2, lead:, :] = v0
        o0, s1 = _ret_chunk(jnp.zeros((RET_DK, RET_DV), F32), pad_ref[h, 0], pad_ref[h, 1], pad_ref[h, 2], *consts[h])
        o_ref[0:N_META, h * LANES:(h + 1) * LANES] = _ret_out(o0[lead:, :], g0, gng_ref[h], gnb_ref[h])
        states.append(s1)

    def body(c, states):
        r0 = pl.multiple_of(N_META + c * RET_CHUNK, 16)
        new = []
        for h in heads:
            q, k, v, g = load(h, r0, RET_CHUNK)
            o, s = _ret_chunk(states[h], q, k, v, *consts[h])
            o_ref[pl.ds(r0, RET_CHUNK), h * LANES:(h + 1) * LANES] = _ret_out(o, g, gng_ref[h], gnb_ref[h])
            new.append(s)
        return tuple(new)

    states = lax.fori_loop(0, n_chunks, body, tuple(states))
    for h in heads:
        st_ref[h] = states[h]


def _retention_prompt(proj_r, cos, sin, consts, gn_g, gn_b, b, t):
    dmat, cdec, kdec, gc = consts
    x = proj_r.reshape(b, t, R_COLS)
    hs, groups = RET_HEADS_PER_STEP, H_RET // RET_HEADS_PER_STEP
    head = lambda part: pl.BlockSpec((None, t, hs * LANES), lambda i, h: (i, 0, part * groups + h))
    tab = pl.BlockSpec((t, LANES), lambda i, h: (0, 0))
    per_h = lambda r: pl.BlockSpec((hs, r, LANES), lambda i, h: (h, 0, 0))
    return pl.pallas_call(
        functools.partial(_ret_kernel, n_chunks=(t - N_META) // RET_CHUNK), grid=(b, groups),
        in_specs=[head(0), head(1), head(2), head(3), tab, tab,
                  per_h(RET_CHUNK), per_h(RET_CHUNK), per_h(RET_CHUNK), per_h(1), per_h(1), per_h(1)],
        out_specs=[pl.BlockSpec((None, t, hs * LANES), lambda i, h: (i, 0, h)),
                   pl.BlockSpec((None, hs, RET_DK, RET_DV), lambda i, h: (i, h, 0, 0))],
        out_shape=[jax.ShapeDtypeStruct((b, t, RET_WIDTH), BF16), jax.ShapeDtypeStruct((b, H_RET, RET_DK, RET_DV), F32)],
        scratch_shapes=[pltpu.VMEM((hs, 3, RET_CHUNK, LANES), F32)],
        compiler_params=_params("parallel", "parallel"), name="retention_prompt",
    )(x, x, x, x, cos, sin, dmat, cdec, kdec, gc, gn_g.reshape(H_RET, 1, RET_DV), gn_b.reshape(H_RET, 1, RET_DV))


def _attn_kernel(q_ref, k_ref, v_ref, o_ref, *, t, blk):
    n_full = t // blk
    starts = [(i * blk, blk) for i in range(n_full)]
    if t % blk:
        starts.append((n_full * blk, t % blk))
    for qi, (q0, qn) in enumerate(starts):
        q = q_ref[q0:q0 + qn, :]
        m = jnp.full((qn, 1), -jnp.inf, F32)
        l = jnp.zeros((qn, 1), F32)
        acc = jnp.zeros((qn, MLA_V), F32)
        for ki, (k0, kn) in enumerate(starts[:qi + 1]):
            s = _dot_nt(q, k_ref[k0:k0 + kn, :]) * SM_SCALE
            if ki == qi:
                rows = lax.broadcasted_iota(jnp.int32, (qn, kn), 0)
                cols = lax.broadcasted_iota(jnp.int32, (qn, kn), 1)
                s = jnp.where(cols <= rows, s, -jnp.inf)
            m_new = jnp.maximum(m, jnp.max(s, -1, keepdims=True))
            p = jnp.exp(s - m_new)
            a = jnp.exp(m - m_new)
            l = a * l + jnp.sum(p, -1, keepdims=True)
            acc = a * acc + _dot(p.astype(BF16), v_ref[k0:k0 + kn, :])
            m = m_new
        o_ref[q0:q0 + qn, :] = (acc / l).astype(BF16)


def _attention_prompt(q, k, v, b, t):
    blk = 512 if t >= 512 else 128
    spec = lambda w: pl.BlockSpec((None, t, w), lambda i, h: (i, 0, h))
    return pl.pallas_call(
        functools.partial(_attn_kernel, t=t, blk=blk), grid=(b, H_MLA),
        in_specs=[spec(HEAD_PAD), spec(HEAD_PAD), spec(MLA_V)], out_specs=spec(MLA_V),
        out_shape=jax.ShapeDtypeStruct((b, t, MLA_WIDTH), BF16),
        compiler_params=_params("parallel", "parallel"), name="attention_prompt",
    )(q.reshape(b, t, -1), k.reshape(b, t, -1), v.reshape(b, t, -1))


def _route(h, wr_hi, wr_lo, bias):
    hi = h.astype(BF16)
    lo = (h - hi.astype(F32)).astype(BF16)
    logits = _dot(hi, wr_hi) + (_dot(lo, wr_hi) + _dot(hi, wr_lo))
    scores = 1.0 / (1.0 + jnp.exp(-logits))
    sel = scores + bias
    lane = lax.broadcasted_iota(jnp.int32, sel.shape, 1).astype(F32)
    neg = -jnp.inf

    def top2(vals):
        t1 = jnp.max(vals, -1, keepdims=True)
        i1 = jnp.min(jnp.where(vals == t1, lane, float(N_EXPERTS)), -1, keepdims=True)
        rest = jnp.where(lane == i1, neg, vals)
        t2 = jnp.max(rest, -1, keepdims=True)
        i2 = jnp.min(jnp.where(rest == t2, lane, float(N_EXPERTS)), -1, keepdims=True)
        return t1, i1, t2, i2

    def in_group(g):
        lo = g * float(EXPERTS_PER_GROUP)
        return jnp.where(lane >= lo, lane, float(N_EXPERTS)) < lo + EXPERTS_PER_GROUP

    best_score = best = None
    for g in range(N_GROUPS):
        t1, _, t2, _ = top2(jnp.where(in_group(float(g)), sel, neg))
        gs = t1 + t2
        if g == 0:
            best_score, best = gs, jnp.zeros_like(gs)
        else:
            upd = gs > best_score
            best_score = jnp.where(upd, gs, best_score)
            best = jnp.where(upd, float(g), best)
    in_best = in_group(best)
    _, i1, _, i2 = top2(jnp.where(in_best, sel, neg))
    s1 = jnp.sum(jnp.where(lane == i1, scores, 0.0), -1, keepdims=True)
    s2 = jnp.sum(jnp.where(lane == i2, scores, 0.0), -1, keepdims=True)
    tot = s1 + s2
    return jnp.where(lane == i1, s1 / tot, 0.0) + jnp.where(lane == i2, s2 / tot, 0.0)


def _oproj_kernel(r_ref, a_ref, h_ref, wa_ref, wb_ref, g_ref, b_ref, whi_ref, wlo_ref, rb_ref,
                  o_ref, ob_ref, gate_ref, *, precise):
    dot = _dot3 if precise else _dot
    y = dot(r_ref[...], wa_ref[...]) + dot(a_ref[...], wb_ref[...])
    h1 = _layer_norm(ALPHA * h_ref[...] + y, g_ref[...], b_ref[...])
    o_ref[...] = h1
    ob_ref[...] = h1.astype(BF16)
    gate_ref[...] = _route(h1, whi_ref[...], wlo_ref[...], rb_ref[...])


def _oproj(ret_o, mla_o, h, w_o, g, b, wr_hi, wr_lo, rbias, layer=None):
    n, d = h.shape
    tm = _tile(n, 512)
    half = w_o.shape[-2] // 2
    row = lambda w: pl.BlockSpec((tm, w), lambda i: (i, 0))
    vec = lambda w: pl.BlockSpec((1, w), lambda i: (0, 0))
    if layer is None:
        wspec = lambda j: pl.BlockSpec((half, d), lambda i: (j, 0))
    else:
        wspec = lambda j: pl.BlockSpec((None, half, d), lambda i: (layer, j, 0))
    rspec = pl.BlockSpec((d, N_EXPERTS), lambda i: (0, 0))
    return pl.pallas_call(
        functools.partial(_oproj_kernel, precise=layer is not None), grid=(n // tm,),
        in_specs=[row(half), row(half), row(d), wspec(0), wspec(1), vec(d), vec(d), rspec, rspec, vec(N_EXPERTS)],
        out_specs=[row(d), row(d), row(N_EXPERTS)],
        out_shape=[jax.ShapeDtypeStruct((n, d), F32), jax.ShapeDtypeStruct((n, d), BF16),
                   jax.ShapeDtypeStruct((n, N_EXPERTS), F32)],
        compiler_params=_params("parallel"), name="oproj_ln1_router",
    )(ret_o, mla_o, h, w_o, w_o, g.reshape(1, d), b.reshape(1, d), wr_hi, wr_lo, rbias.reshape(1, N_EXPERTS))


def _gate_column(gates, e):
    lane = lax.broadcasted_iota(jnp.int32, gates.shape, 1)
    return jnp.sum(jnp.where(lane == e, gates, 0.0), -1, keepdims=True)


def _moe_dense_kernel(x_ref, gate_ref, w1_ref, w3_ref, w2_ref, o_ref, acc_ref):
    e = pl.program_id(1)

    @pl.when(e == 0)
    def _():
        acc_ref[...] = jnp.zeros(acc_ref.shape, F32)

    x = x_ref[...]
    hid = _silu(_dot3(x, w1_ref[...])) * _dot3(x, w3_ref[...])
    acc_ref[...] += _dot3(hid * _gate_column(gate_ref[...], e), w2_ref[...])

    @pl.when(e == N_EXPERTS - 1)
    def _():
        o_ref[...] = acc_ref[...]


def _moe_dense(layer, x, gates, w1, w3, w2):
    n, d = x.shape
    tm = _tile(n, 1024)
    row = lambda w: pl.BlockSpec((tm, w), lambda i, e: (i, 0))
    wspec = lambda r, c: pl.BlockSpec((None, None, r, c), lambda i, e: (layer, e, 0, 0))
    return pl.pallas_call(
        _moe_dense_kernel, grid=(n // tm, N_EXPERTS),
        in_specs=[row(d), row(N_EXPERTS), wspec(d, D_EXPERT), wspec(d, D_EXPERT), wspec(D_EXPERT, d)],
        out_specs=row(d), out_shape=jax.ShapeDtypeStruct((n, d), F32),
        scratch_shapes=[pltpu.VMEM((tm, d), F32)],
        compiler_params=_params("parallel", "arbitrary"), name="moe_dense")(x, gates, w1, w3, w2)


def _moe_sparse_kernel(cnt_ref, x_ref, gate_ref, tri_ref, w1_ref, w3_ref, w2_ref, o_ref, acc_ref, rank_ref):
    i, e = pl.program_id(0), pl.program_id(1)
    tm = x_ref.shape[0]

    @pl.when(e == 0)
    def _():
        acc_ref[...] = jnp.zeros(acc_ref.shape, F32)
        used = jnp.where(gate_ref[...] != 0.0, 1.0, 0.0).astype(BF16)
        rank_ref[...] = _dot(tri_ref[...], used)

    count = cnt_ref[i * N_EXPERTS + e]

    @pl.when(count > 0)
    def _():
        gate = _gate_column(gate_ref[...], e)
        rank = _gate_column(rank_ref[...], e)
        slot = lax.broadcasted_iota(jnp.int32, (1, MOE_CHUNK), 1).astype(F32)

        def chunk(c, carry):
            base = (c * MOE_CHUNK).astype(F32)
            pick = jnp.where((rank == slot + base) & (gate != 0.0), 1.0, 0.0).astype(BF16)
            xs = _dot_tn(pick, x_ref[...]).astype(BF16)
            hid = _silu(_dot(xs, w1_ref[...])) * _dot(xs, w3_ref[...])
            y = _dot(hid.astype(BF16), w2_ref[...])
            acc_ref[...] += gate * _dot(pick, y.astype(BF16))
            return carry

        lax.fori_loop(0, (count + MOE_CHUNK - 1) // MOE_CHUNK, chunk, 0)

    @pl.when(e == N_EXPERTS - 1)
    def _():
        o_ref[...] = acc_ref[...]


def _moe_sparse(layer, xb, gates, w1, w3, w2):
    n, d = xb.shape
    tm = _tile(n, 1024)
    counts = jnp.sum((gates != 0.0).reshape(n // tm, tm, N_EXPERTS), 1, dtype=jnp.int32).reshape(-1)
    tri = jnp.tril(jnp.ones((tm, tm), BF16), -1)
    row = lambda w: pl.BlockSpec((tm, w), lambda i, e, c: (i, 0))
    wspec = lambda r, c: pl.BlockSpec((None, None, r, c), lambda i, e, cnt: (layer, e, 0, 0))
    grid_spec = pltpu.PrefetchScalarGridSpec(
        num_scalar_prefetch=1, grid=(n // tm, N_EXPERTS),
        in_specs=[row(d), row(N_EXPERTS), pl.BlockSpec((tm, tm), lambda i, e, c: (0, 0)),
                  wspec(d, D_EXPERT), wspec(d, D_EXPERT), wspec(D_EXPERT, d)],
        out_specs=row(d),
        scratch_shapes=[pltpu.VMEM((tm, d), F32), pltpu.VMEM((tm, N_EXPERTS), F32)])
    return pl.pallas_call(
        _moe_sparse_kernel, grid_spec=grid_spec, out_shape=jax.ShapeDtypeStruct((n, d), F32),
        compiler_params=_params("parallel", "arbitrary"), name="moe_sparse")(counts, xb, gates, tri, w1, w3, w2)


def _res_ln_kernel(h_ref, y_ref, g_ref, b_ref, o_ref, ob_ref):
    z = _layer_norm(ALPHA * h_ref[...] + y_ref[...], g_ref[...], b_ref[...])
    o_ref[...] = z
    ob_ref[...] = z.astype(BF16)


def _res_ln(h, y, g, b):
    n, d = h.shape
    tm = _tile(n, 1024)
    row = pl.BlockSpec((tm, d), lambda i: (i, 0))
    vec = pl.BlockSpec((1, d), lambda i: (0, 0))
    return pl.pallas_call(
        _res_ln_kernel, grid=(n // tm,), in_specs=[row, row, vec, vec], out_specs=[row, row],
        out_shape=[jax.ShapeDtypeStruct((n, d), F32), jax.ShapeDtypeStruct((n, d), BF16)],
        compiler_params=_params("parallel"), name="residual_ln2")(h, y, g.reshape(1, d), b.reshape(1, d))


def _sample_proj_kernel(x_ref, rc_ref, ra_ref, rb_ref, qg_ref, kg_ref, wuq_ref, wuk_ref,
                        qlat_ref, qpe_ref, ckv_ref, kpe_ref):
    x = x_ref[...]
    rc, ra, rb = rc_ref[...], ra_ref[...], rb_ref[...]
    cq = _rms_norm(x[:, :Q_RANK], qg_ref[...])
    ckv_ref[...] = _rms_norm(x[:, Q_RANK:C_COLS], kg_ref[...])
    kpe_ref[...] = _rope_pad(x[:, C_COLS:], rc, ra, rb)
    q = _dot3(cq, wuq_ref[...])
    for h in range(H_MLA):
        lo = h * HEAD_PAD
        qlat_ref[:, h * KV_RANK:(h + 1) * KV_RANK] = _dot3(q[:, lo:lo + LANES], wuk_ref[h]).astype(BF16)
        qpe_ref[:, h * LANES:(h + 1) * LANES] = _rope_pad(q[:, lo + LANES:lo + HEAD_PAD], rc, ra, rb).astype(BF16)


def _sample_proj(proj_c, rope_c, rope_a, rope_b, qg, kg, wuq, wuk_t):
    n = proj_c.shape[0]
    return pl.pallas_call(
        _sample_proj_kernel,
        out_shape=[jax.ShapeDtypeStruct((n, H_MLA * KV_RANK), BF16), jax.ShapeDtypeStruct((n, H_MLA * LANES), BF16),
                   jax.ShapeDtypeStruct((n, KV_RANK), F32), jax.ShapeDtypeStruct((n, LANES), F32)],
        compiler_params=pltpu.CompilerParams(vmem_limit_bytes=VMEM_LIMIT), name="sample_proj",
    )(proj_c, rope_c, rope_a, rope_b, qg, kg, wuq, wuk_t)


def _ret_step_kernel(x_ref, s_ref, c_ref, sn_ref, gc_ref, gng_ref, gnb_ref, o_ref, st_ref):
    c, sn = c_ref[...], sn_ref[...]
    kscale = RET_DK ** -0.5
    w = H_RET * RET_DK
    rows = 16
    row = lax.broadcasted_iota(jnp.int32, (rows, RET_DK), 0)
    for b in range(x_ref.shape[0]):
        x = x_ref[b]
        for h in range(H_RET):
            q = _rope_half(x[:, h * RET_DK:(h + 1) * RET_DK], c, sn)
            k = _rope_half(x[:, w + h * RET_DK:w + (h + 1) * RET_DK], c, sn) * kscale
            v = x[:, 2 * w + h * RET_DV:2 * w + (h + 1) * RET_DV]
            g = x[:, 3 * w + h * RET_DV:3 * w + (h + 1) * RET_DV]
            k0 = jnp.where(row == 0, k, 0.0)
            s_new = gc_ref[h] * s_ref[b, h] + _dot3(k0, jnp.broadcast_to(v, (rows, RET_DV)), _dot_tn)
            st_ref[b, h] = s_new
            o = _dot3(jnp.broadcast_to(q, (rows, RET_DK)), s_new)[:1, :]
            o_ref[b, :, h * RET_DV:(h + 1) * RET_DV] = _ret_out(o, g, gng_ref[h], gnb_ref[h], F32)


def _retention_step(layer, proj_r, state, cos, sin, gc, gn_g, gn_b):
    bd = proj_r.shape[0]
    nb = _tile(bd, RET_STEP_SEQS, 1)
    full = lambda a: pl.BlockSpec(a.shape, lambda i: (0,) * a.ndim)
    gng, gnb = gn_g.reshape(H_RET, 1, RET_DV), gn_b.reshape(H_RET, 1, RET_DV)
    o, st = pl.pallas_call(
        _ret_step_kernel, grid=(bd // nb,),
        in_specs=[pl.BlockSpec((nb, 1, R_COLS), lambda i: (i, 0, 0)),
                  pl.BlockSpec((None, nb, H_RET, RET_DK, RET_DV), lambda i: (layer, i, 0, 0, 0)),
                  full(cos), full(sin), full(gc), full(gng), full(gnb)],
        out_specs=[pl.BlockSpec((nb, 1, RET_WIDTH), lambda i: (i, 0, 0)),
                   pl.BlockSpec((nb, H_RET, RET_DK, RET_DV), lambda i: (i, 0, 0, 0))],
        out_shape=[jax.ShapeDtypeStruct((bd, 1, RET_WIDTH), F32), jax.ShapeDtypeStruct(state.shape[1:], F32)],
        compiler_params=_params("parallel"), name="retention_step",
    )(proj_r.reshape(bd, 1, R_COLS), state, cos, sin, gc, gng, gnb)
    return o.reshape(bd, RET_WIDTH), st


def _decode_kernel(pt_ref, qlat_ref, qpe_ref, cnew_ref, knew_ref, ckv_hbm, kpe_hbm, o_ref,
                   m_ref, l_ref, acc_ref, cbuf_ref, kbuf_ref, sem, *, layer, pages, n_groups, n_steps):
    i, j = pl.program_id(0), pl.program_id(1)
    step = i * n_groups + j
    slot = step % DECODE_SLOTS
    ahead = DECODE_SLOTS - 1

    def page_copies(seq, group, s):
        copies = []
        for p in range(pages):
            page = pt_ref[seq, group * pages + p]
            copies.append(pltpu.make_async_copy(ckv_hbm.at[layer, page], cbuf_ref.at[s, p], sem.at[s]))
            copies.append(pltpu.make_async_copy(kpe_hbm.at[layer, page], kbuf_ref.at[s, p], sem.at[s]))
        return copies

    @pl.when(step == 0)
    def _():
        for first in range(min(ahead, n_steps)):
            for c in page_copies(first // n_groups, first % n_groups, first):
                c.start()

    later = step + ahead

    @pl.when(later < n_steps)
    def _():
        for c in page_copies(later // n_groups, later % n_groups, later % DECODE_SLOTS):
            c.start()

    for c in page_copies(i, j, slot):
        c.wait()

    @pl.when(j == 0)
    def _():
        m_ref[...] = jnp.full(m_ref.shape, -jnp.inf, F32)
        l_ref[...] = jnp.zeros(l_ref.shape, F32)
        acc_ref[...] = jnp.zeros(acc_ref.shape, F32)

    qlat = qlat_ref[...]
    qpe = qpe_ref[...][:, :QK_ROPE]
    qlat32, qpe32 = qlat.astype(F32), qpe.astype(F32)

    s = jnp.concatenate([_dot_nt(qlat32, cbuf_ref[slot, p]) + _dot(qpe32, kbuf_ref[slot, p])
                         for p in range(pages)], -1) * SM_SCALE
    m_old = m_ref[...]
    m_new = jnp.maximum(m_old, jnp.max(s, -1, keepdims=True))
    p_all = jnp.exp(s - m_new)
    a = jnp.exp(m_old - m_new)
    l_ref[...] = a * l_ref[...] + jnp.sum(p_all, -1, keepdims=True)
    pv = _dot(p_all[:, :PAGE_SIZE], cbuf_ref[slot, 0])
    for p in range(1, pages):
        pv += _dot(p_all[:, p * PAGE_SIZE:(p + 1) * PAGE_SIZE], cbuf_ref[slot, p])
    acc_ref[...] = a * acc_ref[...] + pv
    m_ref[...] = m_new

    @pl.when(j == pl.num_programs(1) - 1)
    def _():
        cn = jnp.broadcast_to(cnew_ref[...], (8, KV_RANK)).astype(BF16)
        kn = jnp.broadcast_to(knew_ref[...][:, :QK_ROPE], (8, QK_ROPE)).astype(BF16)
        s = (_dot_nt(qlat, cn) + _dot_nt(qpe, kn))[:, :1] * SM_SCALE
        m_old = m_ref[...]
        m_new = jnp.maximum(m_old, s)
        p = jnp.exp(s - m_new)
        a = jnp.exp(m_old - m_new)
        l = a * l_ref[...] + p
        acc = a * acc_ref[...] + p.astype(BF16).astype(F32) * cn[:1, :].astype(F32)
        o = acc / l
        for h in range(H_MLA):
            o_ref[:, h * KV_RANK:(h + 1) * KV_RANK] = o[h:h + 1, :]


def _decode_attention(layer, page_table, cache_ckv, cache_kpe_t, qlat, qpe, ckv_new, kpe_new):
    bd, n_pages = page_table.shape
    pages = _tile(n_pages, DECODE_PAGES, 1)
    per_b = lambda r, w: pl.BlockSpec((None, r, w), lambda i, j, pt: (i, 0, 0))
    in_hbm = pl.BlockSpec(memory_space=pl.ANY)
    grid_spec = pltpu.PrefetchScalarGridSpec(
        num_scalar_prefetch=1, grid=(bd, n_pages // pages),
        in_specs=[per_b(H_MLA, KV_RANK), per_b(H_MLA, LANES), per_b(1, KV_RANK), per_b(1, LANES), in_hbm, in_hbm],
        out_specs=per_b(1, H_MLA * KV_RANK),
        scratch_shapes=[pltpu.VMEM((H_MLA, 1), F32), pltpu.VMEM((H_MLA, 1), F32), pltpu.VMEM((H_MLA, KV_RANK), F32),
                        pltpu.VMEM((DECODE_SLOTS, pages, PAGE_SIZE, KV_RANK), F32),
                        pltpu.VMEM((DECODE_SLOTS, pages, QK_ROPE, PAGE_SIZE), F32),
                        pltpu.SemaphoreType.DMA((DECODE_SLOTS,))])
    n_groups = n_pages // pages
    o = pl.pallas_call(
        functools.partial(_decode_kernel, layer=layer, pages=pages, n_groups=n_groups, n_steps=bd * n_groups),
        grid_spec=grid_spec,
        out_shape=jax.ShapeDtypeStruct((bd, 1, H_MLA * KV_RANK), F32),
        compiler_params=_params("arbitrary", "arbitrary"), name="decode_attention",
    )(page_table, qlat.reshape(bd, H_MLA, KV_RANK), qpe.reshape(bd, H_MLA, LANES),
      ckv_new.reshape(bd, 1, KV_RANK), kpe_new.reshape(bd, 1, LANES), cache_ckv, cache_kpe_t)
    return o.reshape(bd, H_MLA * KV_RANK)


def _uv_kernel(o_ref, w_ref, out_ref):
    for h in range(H_MLA):
        out_ref[:, h * MLA_V:(h + 1) * MLA_V] = _dot3(o_ref[:, h * KV_RANK:(h + 1) * KV_RANK], w_ref[h])


def _uv_proj(o_lat, w_uv):
    bd = o_lat.shape[0]
    return pl.pallas_call(
        _uv_kernel, out_shape=jax.ShapeDtypeStruct((bd, MLA_WIDTH), F32),
        compiler_params=pltpu.CompilerParams(vmem_limit_bytes=VMEM_LIMIT), name="uv_proj")(o_lat, w_uv)


def _rope_tables(pos, half):
    inv = ROPE_BASE ** (-jnp.arange(half, dtype=F32) / half)
    ang = pos.astype(F32)[:, None] * inv[None, :]
    return jnp.cos(ang), jnp.sin(ang)


def _rope_half_tables(pos):
    cos, sin = _rope_tables(pos, RET_DK // 2)
    return jnp.concatenate([cos, cos], -1), jnp.concatenate([-sin, sin], -1)


def _rope_pad_tables(pos):
    cos, sin = _rope_tables(pos, QK_ROPE // 2)
    z = jnp.zeros_like(cos)
    zz = jnp.zeros((pos.shape[0], LANES - QK_ROPE), F32)
    return (jnp.concatenate([cos, cos, zz], -1), jnp.concatenate([-sin, z, zz], -1),
            jnp.concatenate([z, sin, zz], -1))


def _retention_consts():
    lg = jnp.log1p(-jnp.exp2(-5.0 - jnp.arange(H_RET, dtype=F32)))
    i = jnp.arange(RET_CHUNK, dtype=F32)
    rel = i[:, None] - i[None, :]
    dmat = jnp.where(rel[None] >= 0, jnp.exp(jnp.maximum(rel, 0.0)[None] * lg[:, None, None]), 0.0)
    cdec = jnp.exp(lg[:, None] * (i + 1.0)[None, :])
    kdec = jnp.exp(lg[:, None] * (RET_CHUNK - 1.0 - i)[None, :])
    wide = lambda a: jnp.broadcast_to(a[:, :, None], (H_RET, RET_CHUNK, LANES))
    gc = jnp.broadcast_to(jnp.exp(lg * RET_CHUNK)[:, None, None], (H_RET, 1, LANES))
    g1 = jnp.broadcast_to(jnp.exp(lg)[:, None, None], (H_RET, 1, LANES))
    return (dmat, wide(cdec), wide(kdec), gc), g1


def _layer_weights(l, w_in_t, w_uq, w_ukv, w_o):
    w_c32 = jnp.pad(w_in_t[l, R_COLS:], ((0, LANES - QK_ROPE), (0, 0)))
    uq = w_uq[l].reshape(Q_RANK, H_MLA, QK_NOPE + QK_ROPE)
    uq32 = jnp.pad(uq, ((0, 0), (0, 0), (0, HEAD_PAD - QK_NOPE - QK_ROPE))).reshape(Q_RANK, H_MLA * HEAD_PAD)
    ukv = w_ukv[l].astype(BF16)
    ukv3 = w_ukv[l].reshape(KV_RANK, H_MLA, QK_NOPE + MLA_V)
    uk_t32 = jnp.transpose(ukv3[..., :QK_NOPE], (1, 2, 0))
    uv32 = jnp.transpose(ukv3[..., QK_NOPE:], (1, 0, 2))
    return dict(w_c=w_c32.astype(BF16), w_c32=w_c32, uq=uq32.astype(BF16), uq32=uq32, ukv=ukv,
                uk_t32=uk_t32, uv32=uv32, w_o=w_o[l].astype(BF16))


def kernel(x_prompt, x_sample, cache_ckv, cache_kpe, state_ret, page_table, meta_tokens, ln0_g, ln0_b, w_in, q_norm_g, w_uq, kv_norm_g, w_ukv, ret_gn_g, ret_gn_b, w_o, ln1_g, ln1_b, w_router, router_bias, w1, w3, w2, ln2_g, ln2_b):
    b, seq, d = x_prompt.shape
    bd = x_sample.shape[0]
    t = seq + N_META
    past_len = page_table.shape[1] * PAGE_SIZE
    depth = w_in.shape[0]

    ret_tab_p = _rope_half_tables(jnp.arange(t))
    ret_tab_s = _rope_half_tables(jnp.full((1,), past_len))
    pad_tab_p = _rope_pad_tables(jnp.arange(t))
    pad_tab_s = _rope_pad_tables(jnp.full((bd,), past_len))
    ret_consts, gamma1 = _retention_consts()
    cache_kpe_t = jnp.swapaxes(cache_kpe, 2, 3)
    w_in_t = jnp.swapaxes(w_in, 1, 2)
    w_in_tb = w_in_t.astype(BF16)
    w1_b, w3_b, w2_b = w1.astype(BF16), w3.astype(BF16), w2.astype(BF16)
    wr_hi = w_router.astype(BF16)
    wr_lo = (w_router - wr_hi.astype(F32)).astype(BF16)

    meta = jnp.broadcast_to(meta_tokens[None], (b, N_META, d))
    xp = jnp.concatenate([meta, x_prompt], 1).reshape(b * t, d)
    hp, hp_b = _ln(xp, ln0_g, ln0_b)
    hs, hs_b = _ln(x_sample.reshape(bd, d), ln0_g, ln0_b)

    outs = [[] for _ in range(6)]
    for l in range(depth):
        w = _layer_weights(l, w_in_t, w_uq, w_ukv, w_o)
        qg, kg = q_norm_g[l].reshape(1, Q_RANK), kv_norm_g[l].reshape(1, KV_RANK)

        proj_r = _mm(hp_b, w_in_tb, 1024, "proj_r", layer=l, ncols=R_COLS)
        proj_c = _mm(hp_b, w["w_c"], C_COLS + LANES, "proj_c")
        q, k, v, ckv, kpe = _mla_proj(proj_c, *pad_tab_p, qg, kg, w["uq"], w["ukv"])
        ret_o, ret_s = _retention_prompt(proj_r, *ret_tab_p, ret_consts, ret_gn_g[l], ret_gn_b[l], b, t)
        mla_o = _attention_prompt(q, k, v, b, t)
        h1, h1_b, gates = _oproj(ret_o.reshape(b * t, -1), mla_o.reshape(b * t, -1), hp, w["w_o"],
                                 ln1_g[l], ln1_b[l], wr_hi, wr_lo, router_bias)
        hp, hp_b = _res_ln(h1, _moe_sparse(l, h1_b, gates, w1_b, w3_b, w2_b), ln2_g[l], ln2_b[l])
        outs[0].append(ckv.reshape(b, t, KV_RANK))
        outs[1].append(kpe[:, :QK_ROPE].reshape(b, t, QK_ROPE))
        outs[2].append(ret_s)

        proj_r = _mm(hs, w_in_t, 1024, "proj_r_s", precise=True, layer=l, ncols=R_COLS)
        proj_c = _mm(hs, w["w_c32"], C_COLS + LANES, "proj_c_s", precise=True)
        qlat, qpe, ckv, kpe = _sample_proj(proj_c, *pad_tab_s, qg, kg, w["uq32"], w["uk_t32"])
        ret_o, ret_s = _retention_step(l, proj_r, state_ret, *ret_tab_s, gamma1, ret_gn_g[l], ret_gn_b[l])
        o_lat = _decode_attention(l, page_table, cache_ckv, cache_kpe_t, qlat, qpe, ckv, kpe)
        mla_o = _uv_proj(o_lat, w["uv32"])
        h1, _, gates = _oproj(ret_o, mla_o, hs, w_o, ln1_g[l], ln1_b[l], wr_hi, wr_lo, router_bias, layer=l)
        hs, _ = _res_ln(h1, _moe_dense(l, h1, gates, w1, w3, w2), ln2_g[l], ln2_b[l])
        outs[3].append(ckv.reshape(bd, 1, KV_RANK))
        outs[4].append(kpe[:, :QK_ROPE].reshape(bd, 1, QK_ROPE))
        outs[5].append(ret_s)

    y_prompt = hp.reshape(b, t, d)[:, N_META:]
    y_sample = hs.reshape(bd, 1, d)
    return (y_prompt, y_sample) + tuple(jnp.stack(o, 0) for o in outs)
```

```python
import functools
import math

import jax
import jax.numpy as jnp
from jax import lax
from jax.experimental import pallas as pl
from jax.experimental.pallas import tpu as pltpu

D_MODEL = 2048
N_META = 16
PAGE_SIZE = 128
H_RET = 8
RET_DK = 128
RET_DV = 128
RET_WIDTH = H_RET * RET_DV
RET_CHUNK = 128
H_MLA = 8
MLA_V = 128
MLA_WIDTH = H_MLA * MLA_V
QK_NOPE = 128
QK_ROPE = 64
Q_RANK = 512
KV_RANK = 512
SM_SCALE = (QK_NOPE + QK_ROPE) ** -0.5
N_EXPERTS = 16
N_GROUPS = 4
EXPERTS_PER_GROUP = N_EXPERTS // N_GROUPS
D_EXPERT = 512
ROPE_BASE = 10000.0
NORM_EPS = 1e-5
DEPTH = 2
ALPHA = (2 * DEPTH) ** 0.25
R_COLS = 2 * H_RET * RET_DK + 2 * RET_WIDTH
C_COLS = Q_RANK + KV_RANK
LANES = 128
HEAD_PAD = 2 * LANES
VMEM_LIMIT = 56 * 1024 * 1024
DECODE_PAGES = 16
DECODE_SLOTS = 4
RET_HEADS_PER_STEP = 2
MOE_CHUNK = 128
RET_STEP_SEQS = 8

BF16 = jnp.bfloat16
F32 = jnp.float32


def _tile(n, cap, align=16):
    best = None
    for t in range(align, min(n, cap) + 1, align):
        if n % t == 0:
            best = t
    return best if best is not None else n


def _params(*sem):
    return pltpu.CompilerParams(dimension_semantics=sem, vmem_limit_bytes=VMEM_LIMIT)


def _dot(a, b):
    return jnp.dot(a, b, preferred_element_type=F32)


def _dot_nt(a, b):
    return lax.dot_general(a, b, (((1,), (1,)), ((), ())), preferred_element_type=F32)


def _dot_tn(a, b):
    return lax.dot_general(a, b, (((0,), (0,)), ((), ())), preferred_element_type=F32)


def _split(x):
    hi = x.astype(BF16)
    return hi, (x - hi.astype(F32)).astype(BF16)


def _dot3(a, b, dot=_dot):
    ah, al = _split(a)
    bh, bl = _split(b)
    return dot(ah, bh) + (dot(al, bh) + dot(ah, bl))


def _layer_norm(x, g, b):
    mu = jnp.mean(x, -1, keepdims=True)
    xc = x - mu
    var = jnp.mean(xc * xc, -1, keepdims=True)
    return xc * lax.rsqrt(var + NORM_EPS) * g + b


def _rms_norm(x, g):
    return x * lax.rsqrt(jnp.mean(x * x, -1, keepdims=True) + NORM_EPS) * g


def _silu(x):
    return x * (1.0 / (1.0 + jnp.exp(-x)))


def _rope_half(x, c, s):
    return x * c + pltpu.roll(x, LANES // 2, 1) * s


def _rope_pad(u, c, a, b):
    return u * c + pltpu.roll(u, LANES - QK_ROPE // 2, 1) * a + pltpu.roll(u, QK_ROPE // 2, 1) * b


def _ln_kernel(x_ref, g_ref, b_ref, o_ref, ob_ref):
    y = _layer_norm(x_ref[...], g_ref[...], b_ref[...])
    o_ref[...] = y
    ob_ref[...] = y.astype(BF16)


def _ln(x, g, b):
    n, d = x.shape
    tm = _tile(n, 1024)
    row = pl.BlockSpec((tm, d), lambda i: (i, 0))
    vec = pl.BlockSpec((1, d), lambda i: (0, 0))
    return pl.pallas_call(
        _ln_kernel, grid=(n // tm,), in_specs=[row, vec, vec], out_specs=[row, row],
        out_shape=[jax.ShapeDtypeStruct((n, d), F32), jax.ShapeDtypeStruct((n, d), BF16)],
        compiler_params=_params("parallel"), name="ln0")(x, g.reshape(1, d), b.reshape(1, d))


def _mm_kernel(x_ref, w_ref, o_ref, *, precise):
    x, w = x_ref[...], w_ref[...]
    o_ref[...] = _dot3(x, w, _dot_nt) if precise else _dot_nt(x, w)


def _mm(x, wt, tn, name, precise=False, layer=None, ncols=None):
    n, k = x.shape
    nw = ncols if ncols is not None else wt.shape[-2]
    tm = _tile(n, 1024)
    if layer is None:
        w_spec = pl.BlockSpec((tn, k), lambda j, i: (j, 0))
    else:
        w_spec = pl.BlockSpec((None, tn, k), lambda j, i: (layer, j, 0))
    return pl.pallas_call(
        functools.partial(_mm_kernel, precise=precise), grid=(nw // tn, n // tm),
        in_specs=[pl.BlockSpec((tm, k), lambda j, i: (i, 0)), w_spec],
        out_specs=pl.BlockSpec((tm, tn), lambda j, i: (i, j)),
        out_shape=jax.ShapeDtypeStruct((n, nw), F32),
        compiler_params=_params("parallel", "parallel"), name=name)(x, wt)


def _mla_proj_kernel(x_ref, rc_ref, ra_ref, rb_ref, qg_ref, kg_ref, wuq_ref, wukv_ref,
                     q_ref, k_ref, v_ref, ckv_ref, kpe_ref):
    x = x_ref[...]
    rc, ra, rb = rc_ref[...], ra_ref[...], rb_ref[...]
    cq = _rms_norm(x[:, :Q_RANK], qg_ref[...]).astype(BF16)
    ckv = _rms_norm(x[:, Q_RANK:C_COLS], kg_ref[...])
    kpe = _rope_pad(x[:, C_COLS:], rc, ra, rb)
    ckv_ref[...] = ckv
    kpe_ref[...] = kpe
    kpe_b = kpe.astype(BF16)
    q = _dot(cq, wuq_ref[...])
    kv = _dot(ckv.astype(BF16), wukv_ref[...])
    for h in range(H_MLA):
        lo = h * HEAD_PAD
        q_ref[:, lo:lo + LANES] = q[:, lo:lo + LANES].astype(BF16)
        q_ref[:, lo + LANES:lo + HEAD_PAD] = _rope_pad(q[:, lo + LANES:lo + HEAD_PAD], rc, ra, rb).astype(BF16)
        k_ref[:, lo:lo + LANES] = kv[:, lo:lo + LANES].astype(BF16)
        k_ref[:, lo + LANES:lo + HEAD_PAD] = kpe_b
        v_ref[:, h * MLA_V:(h + 1) * MLA_V] = kv[:, lo + LANES:lo + HEAD_PAD].astype(BF16)


def _mla_proj(proj_c, rope_c, rope_a, rope_b, qg, kg, wuq, wukv):
    n, t = proj_c.shape[0], rope_c.shape[0]
    tm = _tile(t, 1024)
    row = lambda w: pl.BlockSpec((tm, w), lambda i: (i, 0))
    tab = pl.BlockSpec((tm, LANES), lambda i: (i % (t // tm), 0))
    full = lambda a: pl.BlockSpec(a.shape, lambda i: (0,) * a.ndim)
    return pl.pallas_call(
        _mla_proj_kernel, grid=(n // tm,),
        in_specs=[row(C_COLS + LANES), tab, tab, tab, full(qg), full(kg), full(wuq), full(wukv)],
        out_specs=[row(H_MLA * HEAD_PAD), row(H_MLA * HEAD_PAD), row(MLA_WIDTH), row(KV_RANK), row(LANES)],
        out_shape=[jax.ShapeDtypeStruct((n, H_MLA * HEAD_PAD), BF16), jax.ShapeDtypeStruct((n, H_MLA * HEAD_PAD), BF16),
                   jax.ShapeDtypeStruct((n, MLA_WIDTH), BF16), jax.ShapeDtypeStruct((n, KV_RANK), F32),
                   jax.ShapeDtypeStruct((n, LANES), F32)],
        compiler_params=_params("parallel"), name="mla_proj")(proj_c, rope_c, rope_a, rope_b, qg, kg, wuq, wukv)


def _ret_chunk(s, q, k, v, dmat, cdec, kdec, gc):
    qb, vb = q.astype(BF16), v.astype(BF16)
    inner = _dot((_dot_nt(qb, k.astype(BF16)) * dmat).astype(BF16), vb)
    cross = _dot(qb, s.astype(BF16)) * cdec
    s_new = gc * s + _dot_tn((k * kdec).astype(BF16), vb)
    return inner + cross, s_new


def _ret_out(o, g, gn_g, gn_b, dtype=BF16):
    mu = jnp.mean(o, -1, keepdims=True)
    oc = o - mu
    var = jnp.mean(oc * oc, -1, keepdims=True)
    on = oc * lax.rsqrt(var + NORM_EPS) * gn_g + gn_b
    return (_silu(g) * on).astype(dtype)


def _ret_kernel(q_ref, k_ref, v_ref, g_ref, c_ref, s_ref, dmat_ref, cdec_ref, kdec_ref, gc_ref, gng_ref, gnb_ref,
                o_ref, st_ref, pad_ref, *, n_chunks):
    kscale = RET_DK ** -0.5
    heads = range(RET_HEADS_PER_STEP)
    consts = [(dmat_ref[h], cdec_ref[h], kdec_ref[h], gc_ref[h]) for h in heads]

    def load(h, r0, rows):
        sl, ln = pl.ds(r0, rows), slice(h * LANES, (h + 1) * LANES)
        c, s = c_ref[sl, :], s_ref[sl, :]
        return (_rope_half(q_ref[sl, ln], c, s), _rope_half(k_ref[sl, ln], c, s) * kscale, v_ref[sl, ln], g_ref[sl, ln])

    lead = RET_CHUNK - N_META
    pad_ref[...] = jnp.zeros(pad_ref.shape, F32)
    states = []
    for h in heads:
        q0, k0, v0, g0 = load(h, 0, N_META)
        pad_ref[h, 0, lead:, :] = q0
        pad_ref[h, 1, lead:, :] = k0
        pad_ref[h, 2, lead:, :] = v0
        o0, s1 = _ret_chunk(jnp.zeros((RET_DK, RET_DV), F32), pad_ref[h, 0], pad_ref[h, 1], pad_ref[h, 2], *consts[h])
        o_ref[0:N_META, h * LANES:(h + 1) * LANES] = _ret_out(o0[lead:, :], g0, gng_ref[h], gnb_ref[h])
        states.append(s1)

    def body(c, states):
        r0 = pl.multiple_of(N_META + c * RET_CHUNK, 16)
        new = []
        for h in heads:
            q, k, v, g = load(h, r0, RET_CHUNK)
            o, s = _ret_chunk(states[h], q, k, v, *consts[h])
            o_ref[pl.ds(r0, RET_CHUNK), h * LANES:(h + 1) * LANES] = _ret_out(o, g, gng_ref[h], gnb_ref[h])
            new.append(s)
        return tuple(new)

    states = lax.fori_loop(0, n_chunks, body, tuple(states), unroll=2 if n_chunks % 2 == 0 else 1)
    for h in heads:
        st_ref[h] = states[h]


def _retention_prompt(proj_r, cos, sin, consts, gn_g, gn_b, b, t):
    dmat, cdec, kdec, gc = consts
    x = proj_r.reshape(b, t, R_COLS)
    hs, groups = RET_HEADS_PER_STEP, H_RET // RET_HEADS_PER_STEP
    head = lambda part: pl.BlockSpec((None, t, hs * LANES), lambda i, h: (i, 0, part * groups + h))
    tab = pl.BlockSpec((t, LANES), lambda i, h: (0, 0))
    per_h = lambda r: pl.BlockSpec((hs, r, LANES), lambda i, h: (h, 0, 0))
    return pl.pallas_call(
        functools.partial(_ret_kernel, n_chunks=(t - N_META) // RET_CHUNK), grid=(b, groups),
        in_specs=[head(0), head(1), head(2), head(3), tab, tab,
                  per_h(RET_CHUNK), per_h(RET_CHUNK), per_h(RET_CHUNK), per_h(1), per_h(1), per_h(1)],
        out_specs=[pl.BlockSpec((None, t, hs * LANES), lambda i, h: (i, 0, h)),
                   pl.BlockSpec((None, hs, RET_DK, RET_DV), lambda i, h: (i, h, 0, 0))],
        out_shape=[jax.ShapeDtypeStruct((b, t, RET_WIDTH), BF16), jax.ShapeDtypeStruct((b, H_RET, RET_DK, RET_DV), F32)],
        scratch_shapes=[pltpu.VMEM((hs, 3, RET_CHUNK, LANES), F32)],
        compiler_params=_params("parallel", "parallel"), name="retention_prompt",
    )(x, x, x, x, cos, sin, dmat, cdec, kdec, gc, gn_g.reshape(H_RET, 1, RET_DV), gn_b.reshape(H_RET, 1, RET_DV))


def _attn_kernel(q_ref, k_ref, v_ref, o_ref, *, t, blk):
    n_full = t // blk
    starts = [(i * blk, blk) for i in range(n_full)]
    if t % blk:
        starts.append((n_full * blk, t % blk))
    for qi, (q0, qn) in enumerate(starts):
        q = q_ref[q0:q0 + qn, :]
        m = jnp.full((qn, 1), -jnp.inf, F32)
        l = jnp.zeros((qn, 1), F32)
        acc = jnp.zeros((qn, MLA_V), F32)
        for ki, (k0, kn) in enumerate(starts[:qi + 1]):
            s = _dot_nt(q, k_ref[k0:k0 + kn, :]) * SM_SCALE
            if ki == qi:
                rows = lax.broadcasted_iota(jnp.int32, (qn, kn), 0)
                cols = lax.broadcasted_iota(jnp.int32, (qn, kn), 1)
                s = jnp.where(cols <= rows, s, -jnp.inf)
            m_new = jnp.maximum(m, jnp.max(s, -1, keepdims=True))
            p = jnp.exp(s - m_new)
            a = jnp.exp(m - m_new)
            l = a * l + jnp.sum(p, -1, keepdims=True)
            acc = a * acc + _dot(p.astype(BF16), v_ref[k0:k0 + kn, :])
            m = m_new
        o_ref[q0:q0 + qn, :] = (acc / l).astype(BF16)


def _attention_prompt(q, k, v, b, t):
    blk = 512 if t >= 512 else 128
    spec = lambda w: pl.BlockSpec((None, t, w), lambda i, h: (i, 0, h))
    return pl.pallas_call(
        functools.partial(_attn_kernel, t=t, blk=blk), grid=(b, H_MLA),
        in_specs=[spec(HEAD_PAD), spec(HEAD_PAD), spec(MLA_V)], out_specs=spec(MLA_V),
        out_shape=jax.ShapeDtypeStruct((b, t, MLA_WIDTH), BF16),
        compiler_params=_params("parallel", "parallel"), name="attention_prompt",
    )(q.reshape(b, t, -1), k.reshape(b, t, -1), v.reshape(b, t, -1))


def _route(h, wr_hi, wr_lo, bias):
    hi = h.astype(BF16)
    lo = (h - hi.astype(F32)).astype(BF16)
    logits = _dot(hi, wr_hi) + (_dot(lo, wr_hi) + _dot(hi, wr_lo))
    scores = 1.0 / (1.0 + jnp.exp(-logits))
    sel = scores + bias
    lane = lax.broadcasted_iota(jnp.int32, sel.shape, 1).astype(F32)
    neg = -jnp.inf

    def top2(vals):
        t1 = jnp.max(vals, -1, keepdims=True)
        i1 = jnp.min(jnp.where(vals == t1, lane, float(N_EXPERTS)), -1, keepdims=True)
        rest = jnp.where(lane == i1, neg, vals)
        t2 = jnp.max(rest, -1, keepdims=True)
        i2 = jnp.min(jnp.where(rest == t2, lane, float(N_EXPERTS)), -1, keepdims=True)
        return t1, i1, t2, i2

    def in_group(g):
        lo = g * float(EXPERTS_PER_GROUP)
        return jnp.where(lane >= lo, lane, float(N_EXPERTS)) < lo + EXPERTS_PER_GROUP

    best_score = best = None
    for g in range(N_GROUPS):
        t1, _, t2, _ = top2(jnp.where(in_group(float(g)), sel, neg))
        gs = t1 + t2
        if g == 0:
            best_score, best = gs, jnp.zeros_like(gs)
        else:
            upd = gs > best_score
            best_score = jnp.where(upd, gs, best_score)
            best = jnp.where(upd, float(g), best)
    in_best = in_group(best)
    _, i1, _, i2 = top2(jnp.where(in_best, sel, neg))
    s1 = jnp.sum(jnp.where(lane == i1, scores, 0.0), -1, keepdims=True)
    s2 = jnp.sum(jnp.where(lane == i2, scores, 0.0), -1, keepdims=True)
    tot = s1 + s2
    return jnp.where(lane == i1, s1 / tot, 0.0) + jnp.where(lane == i2, s2 / tot, 0.0)


def _oproj_kernel(r_ref, a_ref, h_ref, wa_ref, wb_ref, g_ref, b_ref, whi_ref, wlo_ref, rb_ref,
                  o_ref, ob_ref, gate_ref, *, precise):
    dot = _dot3 if precise else _dot
    y = dot(r_ref[...], wa_ref[...]) + dot(a_ref[...], wb_ref[...])
    h1 = _layer_norm(ALPHA * h_ref[...] + y, g_ref[...], b_ref[...])
    o_ref[...] = h1
    ob_ref[...] = h1.astype(BF16)
    gate_ref[...] = _route(h1, whi_ref[...], wlo_ref[...], rb_ref[...])


def _oproj(ret_o, mla_o, h, w_o, g, b, wr_hi, wr_lo, rbias, layer=None):
    n, d = h.shape
    tm = _tile(n, 512)
    half = w_o.shape[-2] // 2
    row = lambda w: pl.BlockSpec((tm, w), lambda i: (i, 0))
    vec = lambda w: pl.BlockSpec((1, w), lambda i: (0, 0))
    if layer is None:
        wspec = lambda j: pl.BlockSpec((half, d), lambda i: (j, 0))
    else:
        wspec = lambda j: pl.BlockSpec((None, half, d), lambda i: (layer, j, 0))
    rspec = pl.BlockSpec((d, N_EXPERTS), lambda i: (0, 0))
    return pl.pallas_call(
        functools.partial(_oproj_kernel, precise=layer is not None), grid=(n // tm,),
        in_specs=[row(half), row(half), row(d), wspec(0), wspec(1), vec(d), vec(d), rspec, rspec, vec(N_EXPERTS)],
        out_specs=[row(d), row(d), row(N_EXPERTS)],
        out_shape=[jax.ShapeDtypeStruct((n, d), F32), jax.ShapeDtypeStruct((n, d), BF16),
                   jax.ShapeDtypeStruct((n, N_EXPERTS), F32)],
        compiler_params=_params("parallel"), name="oproj_ln1_router",
    )(ret_o, mla_o, h, w_o, w_o, g.reshape(1, d), b.reshape(1, d), wr_hi, wr_lo, rbias.reshape(1, N_EXPERTS))


def _gate_column(gates, e):
    lane = lax.broadcasted_iota(jnp.int32, gates.shape, 1)
    return jnp.sum(jnp.where(lane == e, gates, 0.0), -1, keepdims=True)


def _moe_dense_kernel(x_ref, gate_ref, w1_ref, w3_ref, w2_ref, o_ref, acc_ref):
    e = pl.program_id(1)

    @pl.when(e == 0)
    def _():
        acc_ref[...] = jnp.zeros(acc_ref.shape, F32)

    x = x_ref[...]
    hid = _silu(_dot3(x, w1_ref[...])) * _dot3(x, w3_ref[...])
    acc_ref[...] += _dot3(hid * _gate_column(gate_ref[...], e), w2_ref[...])

    @pl.when(e == N_EXPERTS - 1)
    def _():
        o_ref[...] = acc_ref[...]


def _moe_dense(layer, x, gates, w1, w3, w2):
    n, d = x.shape
    tm = _tile(n, 1024)
    row = lambda w: pl.BlockSpec((tm, w), lambda i, e: (i, 0))
    wspec = lambda r, c: pl.BlockSpec((None, None, r, c), lambda i, e: (layer, e, 0, 0))
    return pl.pallas_call(
        _moe_dense_kernel, grid=(n // tm, N_EXPERTS),
        in_specs=[row(d), row(N_EXPERTS), wspec(d, D_EXPERT), wspec(d, D_EXPERT), wspec(D_EXPERT, d)],
        out_specs=row(d), out_shape=jax.ShapeDtypeStruct((n, d), F32),
        scratch_shapes=[pltpu.VMEM((tm, d), F32)],
        compiler_params=_params("parallel", "arbitrary"), name="moe_dense")(x, gates, w1, w3, w2)


def _moe_sparse_kernel(cnt_ref, x_ref, gate_ref, tri_ref, w1_ref, w3_ref, w2_ref, o_ref, acc_ref, rank_ref):
    i, e = pl.program_id(0), pl.program_id(1)
    tm = x_ref.shape[0]

    @pl.when(e == 0)
    def _():
        acc_ref[...] = jnp.zeros(acc_ref.shape, F32)
        used = jnp.where(gate_ref[...] != 0.0, 1.0, 0.0).astype(BF16)
        rank_ref[...] = _dot(tri_ref[...], used)

    count = cnt_ref[i * N_EXPERTS + e]

    @pl.when(count > 0)
    def _():
        gate = _gate_column(gate_ref[...], e)
        rank = _gate_column(rank_ref[...], e)
        slot = lax.broadcasted_iota(jnp.int32, (1, MOE_CHUNK), 1).astype(F32)

        def chunk(c, carry):
            base = (c * MOE_CHUNK).astype(F32)
            pick = jnp.where((rank == slot + base) & (gate != 0.0), 1.0, 0.0).astype(BF16)
            xs = _dot_tn(pick, x_ref[...]).astype(BF16)
            hid = _silu(_dot(xs, w1_ref[...])) * _dot(xs, w3_ref[...])
            y = _dot(hid.astype(BF16), w2_ref[...])
            acc_ref[...] += gate * _dot(pick, y.astype(BF16))
            return carry

        lax.fori_loop(0, (count + MOE_CHUNK - 1) // MOE_CHUNK, chunk, 0)

    @pl.when(e == N_EXPERTS - 1)
    def _():
        o_ref[...] = acc_ref[...]


def _moe_sparse(layer, xb, gates, w1, w3, w2):
    n, d = xb.shape
    tm = _tile(n, 1024)
    counts = jnp.sum((gates != 0.0).reshape(n // tm, tm, N_EXPERTS), 1, dtype=jnp.int32).reshape(-1)
    tri = jnp.tril(jnp.ones((tm, tm), BF16), -1)
    row = lambda w: pl.BlockSpec((tm, w), lambda i, e, c: (i, 0))
    wspec = lambda r, c: pl.BlockSpec((None, None, r, c), lambda i, e, cnt: (layer, e, 0, 0))
    grid_spec = pltpu.PrefetchScalarGridSpec(
        num_scalar_prefetch=1, grid=(n // tm, N_EXPERTS),
        in_specs=[row(d), row(N_EXPERTS), pl.BlockSpec((tm, tm), lambda i, e, c: (0, 0)),
                  wspec(d, D_EXPERT), wspec(d, D_EXPERT), wspec(D_EXPERT, d)],
        out_specs=row(d),
        scratch_shapes=[pltpu.VMEM((tm, d), F32), pltpu.VMEM((tm, N_EXPERTS), F32)])
    return pl.pallas_call(
        _moe_sparse_kernel, grid_spec=grid_spec, out_shape=jax.ShapeDtypeStruct((n, d), F32),
        compiler_params=_params("parallel", "arbitrary"), name="moe_sparse")(counts, xb, gates, tri, w1, w3, w2)


def _res_ln_kernel(h_ref, y_ref, g_ref, b_ref, o_ref, ob_ref):
    z = _layer_norm(ALPHA * h_ref[...] + y_ref[...], g_ref[...], b_ref[...])
    o_ref[...] = z
    ob_ref[...] = z.astype(BF16)


def _res_ln(h, y, g, b):
    n, d = h.shape
    tm = _tile(n, 1024)
    row = pl.BlockSpec((tm, d), lambda i: (i, 0))
    vec = pl.BlockSpec((1, d), lambda i: (0, 0))
    return pl.pallas_call(
        _res_ln_kernel, grid=(n // tm,), in_specs=[row, row, vec, vec], out_specs=[row, row],
        out_shape=[jax.ShapeDtypeStruct((n, d), F32), jax.ShapeDtypeStruct((n, d), BF16)],
        compiler_params=_params("parallel"), name="residual_ln2")(h, y, g.reshape(1, d), b.reshape(1, d))


def _sample_proj_kernel(x_ref, rc_ref, ra_ref, rb_ref, qg_ref, kg_ref, wuq_ref, wuk_ref,
                        qlat_ref, qpe_ref, ckv_ref, kpe_ref):
    x = x_ref[...]
    rc, ra, rb = rc_ref[...], ra_ref[...], rb_ref[...]
    cq = _rms_norm(x[:, :Q_RANK], qg_ref[...])
    ckv_ref[...] = _rms_norm(x[:, Q_RANK:C_COLS], kg_ref[...])
    kpe_ref[...] = _rope_pad(x[:, C_COLS:], rc, ra, rb)
    q = _dot3(cq, wuq_ref[...])
    for h in range(H_MLA):
        lo = h * HEAD_PAD
        qlat_ref[:, h * KV_RANK:(h + 1) * KV_RANK] = _dot3(q[:, lo:lo + LANES], wuk_ref[h]).astype(BF16)
        qpe_ref[:, h * LANES:(h + 1) * LANES] = _rope_pad(q[:, lo + LANES:lo + HEAD_PAD], rc, ra, rb).astype(BF16)


def _sample_proj(proj_c, rope_c, rope_a, rope_b, qg, kg, wuq, wuk_t):
    n = proj_c.shape[0]
    return pl.pallas_call(
        _sample_proj_kernel,
        out_shape=[jax.ShapeDtypeStruct((n, H_MLA * KV_RANK), BF16), jax.ShapeDtypeStruct((n, H_MLA * LANES), BF16),
                   jax.ShapeDtypeStruct((n, KV_RANK), F32), jax.ShapeDtypeStruct((n, LANES), F32)],
        compiler_params=pltpu.CompilerParams(vmem_limit_bytes=VMEM_LIMIT), name="sample_proj",
    )(proj_c, rope_c, rope_a, rope_b, qg, kg, wuq, wuk_t)


def _ret_step_kernel(x_ref, s_ref, c_ref, sn_ref, gc_ref, gng_ref, gnb_ref, o_ref, st_ref):
    c, sn = c_ref[...], sn_ref[...]
    kscale = RET_DK ** -0.5
    w = H_RET * RET_DK
    rows = 16
    row = lax.broadcasted_iota(jnp.int32, (rows, RET_DK), 0)
    for b in range(x_ref.shape[0]):
        x = x_ref[b]
        for h in range(H_RET):
            q = _rope_half(x[:, h * RET_DK:(h + 1) * RET_DK], c, sn)
            k = _rope_half(x[:, w + h * RET_DK:w + (h + 1) * RET_DK], c, sn) * kscale
            v = x[:, 2 * w + h * RET_DV:2 * w + (h + 1) * RET_DV]
            g = x[:, 3 * w + h * RET_DV:3 * w + (h + 1) * RET_DV]
            k0 = jnp.where(row == 0, k, 0.0)
            s_new = gc_ref[h] * s_ref[b, h] + _dot3(k0, jnp.broadcast_to(v, (rows, RET_DV)), _dot_tn)
            st_ref[b, h] = s_new
            o = _dot3(jnp.broadcast_to(q, (rows, RET_DK)), s_new)[:1, :]
            o_ref[b, :, h * RET_DV:(h + 1) * RET_DV] = _ret_out(o, g, gng_ref[h], gnb_ref[h], F32)


def _retention_step(layer, proj_r, state, cos, sin, gc, gn_g, gn_b):
    bd = proj_r.shape[0]
    nb = _tile(bd, RET_STEP_SEQS, 1)
    full = lambda a: pl.BlockSpec(a.shape, lambda i: (0,) * a.ndim)
    gng, gnb = gn_g.reshape(H_RET, 1, RET_DV), gn_b.reshape(H_RET, 1, RET_DV)
    o, st = pl.pallas_call(
        _ret_step_kernel, grid=(bd // nb,),
        in_specs=[pl.BlockSpec((nb, 1, R_COLS), lambda i: (i, 0, 0)),
                  pl.BlockSpec((None, nb, H_RET, RET_DK, RET_DV), lambda i: (layer, i, 0, 0, 0)),
                  full(cos), full(sin), full(gc), full(gng), full(gnb)],
        out_specs=[pl.BlockSpec((nb, 1, RET_WIDTH), lambda i: (i, 0, 0)),
                   pl.BlockSpec((nb, H_RET, RET_DK, RET_DV), lambda i: (i, 0, 0, 0))],
        out_shape=[jax.ShapeDtypeStruct((bd, 1, RET_WIDTH), F32), jax.ShapeDtypeStruct(state.shape[1:], F32)],
        compiler_params=_params("parallel"), name="retention_step",
    )(proj_r.reshape(bd, 1, R_COLS), state, cos, sin, gc, gng, gnb)
    return o.reshape(bd, RET_WIDTH), st


def _decode_kernel(pt_ref, qlat_ref, qpe_ref, cnew_ref, knew_ref, ckv_hbm, kpe_hbm, o_ref,
                   m_ref, l_ref, acc_ref, cbuf_ref, kbuf_ref, sem, *, layer, pages, n_groups, n_steps):
    i, j = pl.program_id(0), pl.program_id(1)
    step = i * n_groups + j
    slot = step % DECODE_SLOTS
    ahead = DECODE_SLOTS - 1

    def page_copies(seq, group, s):
        copies = []
        for p in range(pages):
            page = pt_ref[seq, group * pages + p]
            copies.append(pltpu.make_async_copy(ckv_hbm.at[layer, page], cbuf_ref.at[s, p], sem.at[s]))
            copies.append(pltpu.make_async_copy(kpe_hbm.at[layer, page], kbuf_ref.at[s, p], sem.at[s]))
        return copies

    @pl.when(step == 0)
    def _():
        for first in range(min(ahead, n_steps)):
            for c in page_copies(first // n_groups, first % n_groups, first):
                c.start()

    later = step + ahead

    @pl.when(later < n_steps)
    def _():
        for c in page_copies(later // n_groups, later % n_groups, later % DECODE_SLOTS):
            c.start()

    for c in page_copies(i, j, slot):
        c.wait()

    @pl.when(j == 0)
    def _():
        m_ref[...] = jnp.full(m_ref.shape, -jnp.inf, F32)
        l_ref[...] = jnp.zeros(l_ref.shape, F32)
        acc_ref[...] = jnp.zeros(acc_ref.shape, F32)

    qlat = qlat_ref[...]
    qpe = qpe_ref[...][:, :QK_ROPE]
    qlat32, qpe32 = qlat.astype(F32), qpe.astype(F32)

    s = jnp.concatenate([_dot_nt(qlat32, cbuf_ref[slot, p]) + _dot(qpe32, kbuf_ref[slot, p])
                         for p in range(pages)], -1) * SM_SCALE
    m_old = m_ref[...]
    m_new = jnp.maximum(m_old, jnp.max(s, -1, keepdims=True))
    p_all = jnp.exp(s - m_new)
    a = jnp.exp(m_old - m_new)
    l_ref[...] = a * l_ref[...] + jnp.sum(p_all, -1, keepdims=True)
    pv = _dot(p_all[:, :PAGE_SIZE], cbuf_ref[slot, 0])
    for p in range(1, pages):
        pv += _dot(p_all[:, p * PAGE_SIZE:(p + 1) * PAGE_SIZE], cbuf_ref[slot, p])
    acc_ref[...] = a * acc_ref[...] + pv
    m_ref[...] = m_new

    @pl.when(j == pl.num_programs(1) - 1)
    def _():
        cn = jnp.broadcast_to(cnew_ref[...], (8, KV_RANK)).astype(BF16)
        kn = jnp.broadcast_to(knew_ref[...][:, :QK_ROPE], (8, QK_ROPE)).astype(BF16)
        s = (_dot_nt(qlat, cn) + _dot_nt(qpe, kn))[:, :1] * SM_SCALE
        m_old = m_ref[...]
        m_new = jnp.maximum(m_old, s)
        p = jnp.exp(s - m_new)
        a = jnp.exp(m_old - m_new)
        l = a * l_ref[...] + p
        acc = a * acc_ref[...] + p.astype(BF16).astype(F32) * cn[:1, :].astype(F32)
        o = acc / l
        for h in range(H_MLA):
            o_ref[:, h * KV_RANK:(h + 1) * KV_RANK] = o[h:h + 1, :]


def _decode_attention(layer, page_table, cache_ckv, cache_kpe_t, qlat, qpe, ckv_new, kpe_new):
    bd, n_pages = page_table.shape
    pages = _tile(n_pages, DECODE_PAGES, 1)
    per_b = lambda r, w: pl.BlockSpec((None, r, w), lambda i, j, pt: (i, 0, 0))
    in_hbm = pl.BlockSpec(memory_space=pl.ANY)
    grid_spec = pltpu.PrefetchScalarGridSpec(
        num_scalar_prefetch=1, grid=(bd, n_pages // pages),
        in_specs=[per_b(H_MLA, KV_RANK), per_b(H_MLA, LANES), per_b(1, KV_RANK), per_b(1, LANES), in_hbm, in_hbm],
        out_specs=per_b(1, H_MLA * KV_RANK),
        scratch_shapes=[pltpu.VMEM((H_MLA, 1), F32), pltpu.VMEM((H_MLA, 1), F32), pltpu.VMEM((H_MLA, KV_RANK), F32),
                        pltpu.VMEM((DECODE_SLOTS, pages, PAGE_SIZE, KV_RANK), F32),
                        pltpu.VMEM((DECODE_SLOTS, pages, QK_ROPE, PAGE_SIZE), F32),
                        pltpu.SemaphoreType.DMA((DECODE_SLOTS,))])
    n_groups = n_pages // pages
    o = pl.pallas_call(
        functools.partial(_decode_kernel, layer=layer, pages=pages, n_groups=n_groups, n_steps=bd * n_groups),
        grid_spec=grid_spec,
        out_shape=jax.ShapeDtypeStruct((bd, 1, H_MLA * KV_RANK), F32),
        compiler_params=_params("arbitrary", "arbitrary"), name="decode_attention",
    )(page_table, qlat.reshape(bd, H_MLA, KV_RANK), qpe.reshape(bd, H_MLA, LANES),
      ckv_new.reshape(bd, 1, KV_RANK), kpe_new.reshape(bd, 1, LANES), cache_ckv, cache_kpe_t)
    return o.reshape(bd, H_MLA * KV_RANK)


def _uv_kernel(o_ref, w_ref, out_ref):
    for h in range(H_MLA):
        out_ref[:, h * MLA_V:(h + 1) * MLA_V] = _dot3(o_ref[:, h * KV_RANK:(h + 1) * KV_RANK], w_ref[h])


def _uv_proj(o_lat, w_uv):
    bd = o_lat.shape[0]
    return pl.pallas_call(
        _uv_kernel, out_shape=jax.ShapeDtypeStruct((bd, MLA_WIDTH), F32),
        compiler_params=pltpu.CompilerParams(vmem_limit_bytes=VMEM_LIMIT), name="uv_proj")(o_lat, w_uv)


def _rope_tables(pos, half):
    inv = ROPE_BASE ** (-jnp.arange(half, dtype=F32) / half)
    ang = pos.astype(F32)[:, None] * inv[None, :]
    return jnp.cos(ang), jnp.sin(ang)


def _rope_half_tables(pos):
    cos, sin = _rope_tables(pos, RET_DK // 2)
    return jnp.concatenate([cos, cos], -1), jnp.concatenate([-sin, sin], -1)


def _rope_pad_tables(pos):
    cos, sin = _rope_tables(pos, QK_ROPE // 2)
    z = jnp.zeros_like(cos)
    zz = jnp.zeros((pos.shape[0], LANES - QK_ROPE), F32)
    return (jnp.concatenate([cos, cos, zz], -1), jnp.concatenate([-sin, z, zz], -1),
            jnp.concatenate([z, sin, zz], -1))


def _retention_consts():
    lg = jnp.log1p(-jnp.exp2(-5.0 - jnp.arange(H_RET, dtype=F32)))
    i = jnp.arange(RET_CHUNK, dtype=F32)
    rel = i[:, None] - i[None, :]
    dmat = jnp.where(rel[None] >= 0, jnp.exp(jnp.maximum(rel, 0.0)[None] * lg[:, None, None]), 0.0)
    cdec = jnp.exp(lg[:, None] * (i + 1.0)[None, :])
    kdec = jnp.exp(lg[:, None] * (RET_CHUNK - 1.0 - i)[None, :])
    wide = lambda a: jnp.broadcast_to(a[:, :, None], (H_RET, RET_CHUNK, LANES))
    gc = jnp.broadcast_to(jnp.exp(lg * RET_CHUNK)[:, None, None], (H_RET, 1, LANES))
    g1 = jnp.broadcast_to(jnp.exp(lg)[:, None, None], (H_RET, 1, LANES))
    return (dmat, wide(cdec), wide(kdec), gc), g1


def _layer_weights(l, w_in_t, w_uq, w_ukv, w_o):
    w_c32 = jnp.pad(w_in_t[l, R_COLS:], ((0, LANES - QK_ROPE), (0, 0)))
    uq = w_uq[l].reshape(Q_RANK, H_MLA, QK_NOPE + QK_ROPE)
    uq32 = jnp.pad(uq, ((0, 0), (0, 0), (0, HEAD_PAD - QK_NOPE - QK_ROPE))).reshape(Q_RANK, H_MLA * HEAD_PAD)
    ukv = w_ukv[l].astype(BF16)
    ukv3 = w_ukv[l].reshape(KV_RANK, H_MLA, QK_NOPE + MLA_V)
    uk_t32 = jnp.transpose(ukv3[..., :QK_NOPE], (1, 2, 0))
    uv32 = jnp.transpose(ukv3[..., QK_NOPE:], (1, 0, 2))
    return dict(w_c=w_c32.astype(BF16), w_c32=w_c32, uq=uq32.astype(BF16), uq32=uq32, ukv=ukv,
                uk_t32=uk_t32, uv32=uv32, w_o=w_o[l].astype(BF16))


def kernel(x_prompt, x_sample, cache_ckv, cache_kpe, state_ret, page_table, meta_tokens, ln0_g, ln0_b, w_in, q_norm_g, w_uq, kv_norm_g, w_ukv, ret_gn_g, ret_gn_b, w_o, ln1_g, ln1_b, w_router, router_bias, w1, w3, w2, ln2_g, ln2_b):
    b, seq, d = x_prompt.shape
    bd = x_sample.shape[0]
    t = seq + N_META
    past_len = page_table.shape[1] * PAGE_SIZE
    depth = w_in.shape[0]

    ret_tab_p = _rope_half_tables(jnp.arange(t))
    ret_tab_s = _rope_half_tables(jnp.full((1,), past_len))
    pad_tab_p = _rope_pad_tables(jnp.arange(t))
    pad_tab_s = _rope_pad_tables(jnp.full((bd,), past_len))
    ret_consts, gamma1 = _retention_consts()
    cache_kpe_t = jnp.swapaxes(cache_kpe, 2, 3)
    w_in_t = jnp.swapaxes(w_in, 1, 2)
    w_in_tb = w_in_t.astype(BF16)
    w1_b, w3_b, w2_b = w1.astype(BF16), w3.astype(BF16), w2.astype(BF16)
    wr_hi = w_router.astype(BF16)
    wr_lo = (w_router - wr_hi.astype(F32)).astype(BF16)

    meta = jnp.broadcast_to(meta_tokens[None], (b, N_META, d))
    xp = jnp.concatenate([meta, x_prompt], 1).reshape(b * t, d)
    hp, hp_b = _ln(xp, ln0_g, ln0_b)
    hs, hs_b = _ln(x_sample.reshape(bd, d), ln0_g, ln0_b)

    outs = [[] for _ in range(6)]
    for l in range(depth):
        w = _layer_weights(l, w_in_t, w_uq, w_ukv, w_o)
        qg, kg = q_norm_g[l].reshape(1, Q_RANK), kv_norm_g[l].reshape(1, KV_RANK)

        proj_r = _mm(hp_b, w_in_tb, 1024, "proj_r", layer=l, ncols=R_COLS)
        proj_c = _mm(hp_b, w["w_c"], C_COLS + LANES, "proj_c")
        q, k, v, ckv, kpe = _mla_proj(proj_c, *pad_tab_p, qg, kg, w["uq"], w["ukv"])
        ret_o, ret_s = _retention_prompt(proj_r, *ret_tab_p, ret_consts, ret_gn_g[l], ret_gn_b[l], b, t)
        mla_o = _attention_prompt(q, k, v, b, t)
        h1, h1_b, gates = _oproj(ret_o.reshape(b * t, -1), mla_o.reshape(b * t, -1), hp, w["w_o"],
                                 ln1_g[l], ln1_b[l], wr_hi, wr_lo, router_bias)
        hp, hp_b = _res_ln(h1, _moe_sparse(l, h1_b, gates, w1_b, w3_b, w2_b), ln2_g[l], ln2_b[l])
        outs[0].append(ckv.reshape(b, t, KV_RANK))
        outs[1].append(kpe[:, :QK_ROPE].reshape(b, t, QK_ROPE))
        outs[2].append(ret_s)

        proj_r = _mm(hs, w_in_t, 1024, "proj_r_s", precise=True, layer=l, ncols=R_COLS)
        proj_c = _mm(hs, w["w_c32"], C_COLS + LANES, "proj_c_s", precise=True)
        qlat, qpe, ckv, kpe = _sample_proj(proj_c, *pad_tab_s, qg, kg, w["uq32"], w["uk_t32"])
        ret_o, ret_s = _retention_step(l, proj_r, state_ret, *ret_tab_s, gamma1, ret_gn_g[l], ret_gn_b[l])
        o_lat = _decode_attention(l, page_table, cache_ckv, cache_kpe_t, qlat, qpe, ckv, kpe)
        mla_o = _uv_proj(o_lat, w["uv32"])
        h1, _, gates = _oproj(ret_o, mla_o, hs, w_o, ln1_g[l], ln1_b[l], wr_hi, wr_lo, router_bias, layer=l)
        hs, _ = _res_ln(h1, _moe_dense(l, h1, gates, w1, w3, w2), ln2_g[l], ln2_b[l])
        outs[3].append(ckv.reshape(bd, 1, KV_RANK))
        outs[4].append(kpe[:, :QK_ROPE].reshape(bd, 1, QK_ROPE))
        outs[5].append(ret_s)

    y_prompt = hp.reshape(b, t, d)[:, N_META:]
    y_sample = hs.reshape(bd, 1, d)
    return (y_prompt, y_sample) + tuple(jnp.stack(o, 0) for o in outs)
```
